```python
import math
import jax, jax.numpy as jnp
from jax import lax
import numpy as np

D_MODEL = 1024
BATCH = 1
SEQ = 16384
DEPTH = 1
DEC_BATCH = 128
DEC_SEQ = 8
PAST_LEN = 8192
PAGE_SIZE = 128

H_A = 8
KV_A = 2
G_A = H_A // KV_A
HD_A = 64
DV_A = 2 * HD_A
H_B = 8
KV_B = 2
G_B = H_B // KV_B
HD_B = 128
H_IDX = 8
D_IDX = 64
TOPK_MAX = 256
D_FF = 4 * D_MODEL
ROPE_THETA = 10000.0
Q_BLOCK = 128
EPS = 1e-6

SPLITS = (H_A * 2 * HD_A, KV_A * 2 * HD_A, KV_A * DV_A, H_B * HD_B, KV_B * HD_B, KV_B * HD_B,
          H_IDX * D_IDX, D_IDX, H_IDX, D_MODEL, D_MODEL)
D_IN = sum(SPLITS)

kernel_name = 'hybrid_diffattn_dsa_decode_step'


def rms_norm(x, g):
    xf = x.astype(jnp.float32)
    y = xf * lax.rsqrt(jnp.mean(xf * xf, axis=-1, keepdims=True) + EPS)
    return (y * g.astype(jnp.float32)).astype(x.dtype)


def rope(x, pos):
    half = x.shape[-1] // 2
    inv_freq = ROPE_THETA ** (-jnp.arange(half, dtype=jnp.float32) / half)
    ang = pos.astype(jnp.float32)[:, None] * inv_freq[None, :]
    shape = (1, pos.shape[0]) + (1,) * (x.ndim - 3) + (half,)
    cos = jnp.cos(ang).reshape(shape)
    sin = jnp.sin(ang).reshape(shape)
    xf = x.astype(jnp.float32)
    x1, x2 = xf[..., :half], xf[..., half:]
    return jnp.concatenate([x1 * cos - x2 * sin, x2 * cos + x1 * sin], axis=-1).astype(x.dtype)


def lambda_init(layer):
    return 0.8 - 0.6 * math.exp(-0.3 * layer)


def project(xn, w_in, pos, q_norm_a, k_norm_a, q_norm_b, k_norm_b, idx_k_norm):
    n, t, _ = xn.shape
    h = xn @ w_in
    offs = [int(o) for o in np.cumsum(SPLITS)[:-1]]
    qa, ka, va, qb, kb, vb, qi, ki, wi, ga, gb = jnp.split(h, offs, axis=-1)
    qa = rope(rms_norm(qa.reshape(n, t, KV_A, G_A, 2, HD_A), q_norm_a), pos)
    ka = rope(rms_norm(ka.reshape(n, t, KV_A, 2, HD_A), k_norm_a), pos)
    va = va.reshape(n, t, KV_A, DV_A)
    qb = rope(rms_norm(qb.reshape(n, t, KV_B, G_B, HD_B), q_norm_b), pos)
    kb = rope(rms_norm(kb.reshape(n, t, KV_B, HD_B), k_norm_b), pos)
    vb = vb.reshape(n, t, KV_B, HD_B)
    qi = rope(qi.reshape(n, t, H_IDX, D_IDX), pos)
    ki = rope(rms_norm(ki, idx_k_norm), pos)
    return qa, ka, va, qb, kb, vb, qi, ki, wi, ga, gb


def to_blocks(a, qbs):
    n, t = a.shape[:2]
    return jnp.moveaxis(a.reshape((n, t // qbs, qbs) + a.shape[2:]), 1, 0)


def from_blocks(a):
    nb, n, qbs = a.shape[:3]
    return jnp.moveaxis(a, 0, 1).reshape((n, nb * qbs) + a.shape[3:])


def diff_attn(qa, ka, va, q_pos, k_pos, lam):
    s = jnp.einsum('nqkgcd,nskcd->nkgcqs', qa, ka).astype(jnp.float32) * (HD_A ** -0.5)
    mask = k_pos[None, :] <= q_pos[:, None]
    p = jax.nn.softmax(jnp.where(mask, s, -jnp.inf), axis=-1)
    pd = p[:, :, :, 0] - lam * p[:, :, :, 1]
    return jnp.einsum('nkgqs,nskv->nqkgv', pd, va.astype(jnp.float32))


def indexer_topk(qi, wi, ki, q_pos, topk):
    logits = jnp.einsum('nqhd,nsd->nqhs', qi, ki).astype(jnp.float32) * (D_IDX ** -0.5)
    score = jnp.einsum('nqh,nqhs->nqs', wi.astype(jnp.float32) * (H_IDX ** -0.5), jax.nn.relu(logits))
    k_pos = jnp.arange(ki.shape[1])
    score = jnp.where(k_pos[None, None, :] <= q_pos[None, :, None], score, -jnp.inf)
    _, idx = lax.top_k(score, topk)
    valid = idx <= q_pos[None, :, None]
    return idx, valid


def sparse_attn(qb, k_sel, v_sel, valid):
    s = jnp.einsum('nqkgd,nqjkd->nqkgj', qb, k_sel).astype(jnp.float32) * (HD_B ** -0.5)
    p = jax.nn.softmax(jnp.where(valid[:, :, None, None, :], s, -jnp.inf), axis=-1)
    return jnp.einsum('nqkgj,nqjkd->nqkgd', p, v_sel.astype(jnp.float32))


def prompt_diff_attn(qa, ka, va, lam):
    t = qa.shape[1]
    qbs = min(Q_BLOCK, t)
    pos = jnp.arange(t)

    def body(args):
        q_blk, p_blk = args
        return diff_attn(q_blk, ka, va, p_blk, pos, lam)

    return from_blocks(lax.map(body, (to_blocks(qa, qbs), pos.reshape(-1, qbs))))


def prompt_sparse_attn(qb, kb, vb, qi, ki, wi):
    n, t = qb.shape[:2]
    qbs = min(Q_BLOCK, t)
    topk = min(TOPK_MAX, t // 4)
    pos = jnp.arange(t)
    rows = jnp.arange(n)[:, None, None]

    def body(args):
        q_blk, qi_blk, wi_blk, p_blk = args
        idx, valid = indexer_topk(qi_blk, wi_blk, ki, p_blk, topk)
        return sparse_attn(q_blk, kb[rows, idx], vb[rows, idx], valid)

    out = lax.map(body, (to_blocks(qb, qbs), to_blocks(qi, qbs), to_blocks(wi, qbs), pos.reshape(-1, qbs)))
    return from_blocks(out)


def gather_past(pool, layer, page_table):
    nb, n_pages = page_table.shape
    return pool[layer, page_table].reshape((nb, n_pages * PAGE_SIZE) + pool.shape[3:])


def gather_rows(pool, layer, new, idx, page_table):
    rows = jnp.arange(idx.shape[0])[:, None, None]
    past = page_table.shape[1] * PAGE_SIZE
    pidx = jnp.minimum(idx, past - 1)
    from_past = pool[layer, page_table[rows, pidx // PAGE_SIZE], pidx % PAGE_SIZE]
    from_new = new[rows, jnp.clip(idx - past, 0, new.shape[1] - 1)].astype(from_past.dtype)
    is_past = (idx < past).reshape(idx.shape + (1,) * (new.ndim - 2))
    return jnp.where(is_past, from_past, from_new)


def sample_diff_attn(qa, ka, va, pool_k, pool_v, layer, page_table, lam):
    tn = qa.shape[1]
    past = page_table.shape[1] * PAGE_SIZE
    k_past = gather_past(pool_k, layer, page_table)
    v_past = gather_past(pool_v, layer, page_table)
    ka_all = jnp.concatenate([k_past, ka.astype(k_past.dtype)], axis=1)
    va_all = jnp.concatenate([v_past, va.astype(v_past.dtype)], axis=1)
    q_pos = past + jnp.arange(tn)
    k_pos = jnp.arange(past + tn)
    return diff_attn(qa, ka_all, va_all, q_pos, k_pos, lam)


def sample_sparse_attn(qb, kb, vb, qi, ki, wi, pool_k, pool_v, pool_idx, layer, page_table):
    tn = qb.shape[1]
    past = page_table.shape[1] * PAGE_SIZE
    topk = min(TOPK_MAX, (past + tn) // 4)
    ki_past = gather_past(pool_idx, layer, page_table)
    ki_all = jnp.concatenate([ki_past, ki.astype(ki_past.dtype)], axis=1)
    q_pos = past + jnp.arange(tn)
    idx, valid = indexer_topk(qi, wi, ki_all, q_pos, topk)
    k_sel = gather_rows(pool_k, layer, kb, idx, page_table)
    v_sel = gather_rows(pool_v, layer, vb, idx, page_table)
    return sparse_attn(qb, k_sel, v_sel, valid)


def finish_layer(x, oa, ob, ga, gb, sub_norm_a, lam_init, w_a_proj, w_b_proj, w_out, g_mlp_norm, w_up, w_down):
    n, t = x.shape[:2]
    oa = rms_norm(oa.reshape(n, t, H_A, DV_A), sub_norm_a) * (1.0 - lam_init)
    pa = oa.reshape(n, t, H_A * DV_A).astype(x.dtype) @ w_a_proj
    pb = ob.reshape(n, t, H_B * HD_B).astype(x.dtype) @ w_b_proj
    x = x + (jax.nn.sigmoid(ga) * pa + jax.nn.sigmoid(gb) * pb) @ w_out
    h = jax.nn.relu(rms_norm(x, g_mlp_norm) @ w_up)
    return x + (h * h) @ w_down


def setup_inputs(seed: int = 0) -> dict:
    key = jax.random.key(seed)
    ks = jax.random.split(key, 26)
    f32 = jnp.float32
    n_pages = PAST_LEN // PAGE_SIZE
    n_used = DEC_BATCH * n_pages
    n_phys = n_used + max(1, n_used // 4)
    pool = (DEPTH, n_phys, PAGE_SIZE)

    def nrm(k, shape, scale=1.0):
        return jax.random.normal(k, shape, f32) * scale

    def gain(k, n):
        return 1.0 + 0.01 * jax.random.normal(k, (DEPTH, n), f32)

    page_table = jax.random.permutation(ks[7], n_phys)[:n_used].reshape(DEC_BATCH, n_pages).astype(jnp.int32)
    return {
        'x_prompt': nrm(ks[0], (BATCH, SEQ, D_MODEL)),
        'x_sample': nrm(ks[1], (DEC_BATCH, DEC_SEQ, D_MODEL)),
        'cache_a_k': nrm(ks[2], pool + (KV_A, 2, HD_A)),
        'cache_a_v': nrm(ks[3], pool + (KV_A, DV_A)),
        'cache_b_k': nrm(ks[4], pool + (KV_B, HD_B)),
        'cache_b_v': nrm(ks[5], pool + (KV_B, HD_B)),
        'cache_b_idx_k': nrm(ks[6], pool + (D_IDX,)),
        'page_table': page_table,
        'g_attn_norm': gain(ks[8], D_MODEL),
        'w_in': nrm(ks[9], (DEPTH, D_MODEL, D_IN), D_MODEL ** -0.5),
        'q_norm_a': gain(ks[10], HD_A),
        'k_norm_a': gain(ks[11], HD_A),
        'lam_q1': nrm(ks[12], (DEPTH, HD_A), 0.1),
        'lam_k1': nrm(ks[13], (DEPTH, HD_A), 0.1),
        'lam_q2': nrm(ks[14], (DEPTH, HD_A), 0.1),
        'lam_k2': nrm(ks[15], (DEPTH, HD_A), 0.1),
        'sub_norm_a': gain(ks[16], DV_A),
        'q_norm_b': gain(ks[17], HD_B),
        'k_norm_b': gain(ks[18], HD_B),
        'idx_k_norm': gain(ks[19], D_IDX),
        'w_a_proj': nrm(ks[20], (DEPTH, H_A * DV_A, D_MODEL), (H_A * DV_A) ** -0.5),
        'w_b_proj': nrm(ks[21], (DEPTH, H_B * HD_B, D_MODEL), (H_B * HD_B) ** -0.5),
        'w_out': nrm(ks[22], (DEPTH, D_MODEL, D_MODEL), D_MODEL ** -0.5),
        'g_mlp_norm': gain(ks[23], D_MODEL),
        'w_up': nrm(ks[24], (DEPTH, D_MODEL, D_FF), D_MODEL ** -0.5),
        'w_down': nrm(ks[25], (DEPTH, D_FF, D_MODEL), 0.5 * D_FF ** -0.5),
    }


def reference(x_prompt, x_sample, cache_a_k, cache_a_v, cache_b_k, cache_b_v, cache_b_idx_k, page_table,
              g_attn_norm, w_in, q_norm_a, k_norm_a, lam_q1, lam_k1, lam_q2, lam_k2, sub_norm_a,
              q_norm_b, k_norm_b, idx_k_norm, w_a_proj, w_b_proj, w_out, g_mlp_norm, w_up, w_down):
    f32 = jnp.float32
    past = page_table.shape[1] * PAGE_SIZE
    pos_p = jnp.arange(x_prompt.shape[1])
    pos_s = past + jnp.arange(x_sample.shape[1])
    xp, xs = x_prompt, x_sample
    akp, avp, bkp, bvp, ikp = [], [], [], [], []
    aks, avs, bks, bvs, iks = [], [], [], [], []
    for l in range(DEPTH):
        lam_init = lambda_init(l)
        lam = (jnp.exp(jnp.sum(lam_q1[l].astype(f32) * lam_k1[l].astype(f32)))
               - jnp.exp(jnp.sum(lam_q2[l].astype(f32) * lam_k2[l].astype(f32))) + lam_init)
        qa, ka, va, qb, kb, vb, qi, ki, wi, ga, gb = project(
            rms_norm(xp, g_attn_norm[l]), w_in[l], pos_p,
            q_norm_a[l], k_norm_a[l], q_norm_b[l], k_norm_b[l], idx_k_norm[l])
        oa = prompt_diff_attn(qa, ka, va, lam)
        ob = prompt_sparse_attn(qb, kb, vb, qi, ki, wi)
        xp = finish_layer(xp, oa, ob, ga, gb, sub_norm_a[l], lam_init, w_a_proj[l], w_b_proj[l],
                          w_out[l], g_mlp_norm[l], w_up[l], w_down[l])
        akp.append(ka); avp.append(va); bkp.append(kb); bvp.append(vb); ikp.append(ki)
        qa, ka, va, qb, kb, vb, qi, ki, wi, ga, gb = project(
            rms_norm(xs, g_attn_norm[l]), w_in[l], pos_s,
            q_norm_a[l], k_norm_a[l], q_norm_b[l], k_norm_b[l], idx_k_norm[l])
        oa = sample_diff_attn(qa, ka, va, cache_a_k, cache_a_v, l, page_table, lam)
        ob = sample_sparse_attn(qb, kb, vb, qi, ki, wi, cache_b_k, cache_b_v, cache_b_idx_k, l, page_table)
        xs = finish_layer(xs, oa, ob, ga, gb, sub_norm_a[l], lam_init, w_a_proj[l], w_b_proj[l],
                          w_out[l], g_mlp_norm[l], w_up[l], w_down[l])
        aks.append(ka); avs.append(va); bks.append(kb); bvs.append(vb); iks.append(ki)
    a_k_p = jnp.stack(akp)
    a_v_p = jnp.stack(avp)
    b_k_p = jnp.stack(bkp)
    b_v_p = jnp.stack(bvp)
    idx_k_p = jnp.stack(ikp)
    a_k_s = jnp.stack(aks)
    a_v_s = jnp.stack(avs)
    b_k_s = jnp.stack(bks)
    b_v_s = jnp.stack(bvs)
    idx_k_s = jnp.stack(iks)
    return (xp, xs, a_k_p, a_v_p, b_k_p, b_v_p, idx_k_p, a_k_s, a_v_s, b_k_s, b_v_s, idx_k_s)
```

```python
import functools
import math

import jax
import jax.numpy as jnp
import numpy as np
from jax import lax
from jax.experimental import pallas as pl
from jax.experimental.pallas import tpu as pltpu

F32 = jnp.float32
BF16 = jnp.bfloat16
I32 = jnp.int32

LANES = 128
VMEM_LIMIT = 56 * 1024 * 1024
EPS = 1e-6
ROPE_THETA = 10000.0
TOPK_MAX = 256
NEG_BIG = -1e30
INT_MIN = -(2 ** 31)

H_A, KV_A, HD_A = 8, 2, 64
G_A = H_A // KV_A
DV_A = 2 * HD_A
H_B, KV_B, HD_B = 8, 2, 128
G_B = H_B // KV_B
H_IDX, D_IDX = 8, 64
W_QA = H_A * 2 * HD_A
W_KA = KV_A * 2 * HD_A
W_VA = KV_A * DV_A
W_QB = H_B * HD_B
W_KB = KV_B * HD_B
W_QI = H_IDX * D_IDX


def _cparams(sem):
    return pltpu.CompilerParams(dimension_semantics=sem, vmem_limit_bytes=VMEM_LIMIT)


def _const_spec(shape):
    nd = len(shape)
    return pl.BlockSpec(shape, lambda *_: (0,) * nd, pipeline_mode=pl.Buffered(1))


def _dot_t(a, b):
    return lax.dot_general(a, b, (((1,), (1,)), ((), ())), preferred_element_type=F32)


def _dot(a, b):
    return jnp.dot(a, b, preferred_element_type=F32)


def _group_rms(h, ones_ref, inv_group):
    outs = []
    for c in range(h.shape[1] // LANES):
        hs = h[:, c * LANES:(c + 1) * LANES]
        sq = hs * hs
        hi = sq.astype(BF16)
        lo = (sq - hi.astype(F32)).astype(BF16)
        ss = _dot(hi, ones_ref[...]) + _dot(lo, ones_ref[...])
        outs.append(hs * lax.rsqrt(ss * inv_group + EPS))
    return outs


def _rope_chunks(chunks, gain, cos, sin, half):
    lane = lax.broadcasted_iota(I32, chunks[0].shape, 1)
    first = (lane % (2 * half)) < half
    outs = []
    for c, y in enumerate(chunks):
        y = y * gain[:, c * LANES:(c + 1) * LANES]
        if half == LANES // 2:
            partner = pltpu.roll(y, half, 1)
        else:
            partner = jnp.where(first, pltpu.roll(y, LANES - half, 1), pltpu.roll(y, half, 1))
        outs.append(y * cos + partner * sin)
    return outs


def _proj_kernel(x_ref, g_ref, wqa, wka, wva, wqb, wkb, wvb, wqi, wkw, wga, wgb,
                 gqa, gka, gqb, gkb, gki, cos_a, sin_a, cos_b, sin_b, ones64, ones128,
                 qa_o, kaf_o, kab_o, vaf_o, vab_o, qb_o, kbf_o, kbb_o, vbf_o, vbb_o,
                 qi_o, kif_o, kib_o, wi_o, ga_o, gb_o):
    x = x_ref[...]
    ms = jnp.mean(x * x, axis=-1, keepdims=True)
    xn = ((x * lax.rsqrt(ms + EPS)) * g_ref[...]).astype(BF16)
    ca, sa, cb, sb = cos_a[...], sin_a[...], cos_b[...], sin_b[...]

    def cat(chunks):
        return chunks[0] if len(chunks) == 1 else jnp.concatenate(chunks, axis=1)

    h = _dot(xn, wqa[...])
    qa_o[...] = cat(_rope_chunks(_group_rms(h, ones64, 1.0 / HD_A), gqa[...], ca, sa, HD_A // 2)).astype(BF16)
    h = _dot(xn, wka[...])
    ka = cat(_rope_chunks(_group_rms(h, ones64, 1.0 / HD_A), gka[...], ca, sa, HD_A // 2))
    kaf_o[...] = ka
    kab_o[...] = ka.astype(BF16)
    h = _dot(xn, wva[...])
    vaf_o[...] = h
    vab_o[...] = h.astype(BF16)
    h = _dot(xn, wqb[...])
    qb_o[...] = cat(_rope_chunks(_group_rms(h, ones128, 1.0 / HD_B), gqb[...], cb, sb, HD_B // 2)).astype(BF16)
    h = _dot(xn, wkb[...])
    kb = cat(_rope_chunks(_group_rms(h, ones128, 1.0 / HD_B), gkb[...], cb, sb, HD_B // 2))
    kbf_o[...] = kb
    kbb_o[...] = kb.astype(BF16)
    h = _dot(xn, wvb[...])
    vbf_o[...] = h
    vbb_o[...] = h.astype(BF16)
    h = _dot(xn, wqi[...])
    chunks = [h[:, c * LANES:(c + 1) * LANES] for c in range(W_QI // LANES)]
    ones_gain = jnp.full((1, W_QI), D_IDX ** -0.5, F32)
    qi_o[...] = cat(_rope_chunks(chunks, ones_gain, ca, sa, D_IDX // 2)).astype(BF16)
    h = _dot(xn, wkw[...])
    ki = _rope_chunks(_group_rms(h, ones64, 1.0 / D_IDX), gki[...], ca, sa, D_IDX // 2)[0]
    kif_o[...] = ki[:, :D_IDX]
    kib_o[...] = ki[:, :D_IDX].astype(BF16)
    wi_o[...] = h * (H_IDX ** -0.5)
    ga_o[...] = jax.nn.sigmoid(_dot(xn, wga[...]))
    gb_o[...] = jax.nn.sigmoid(_dot(xn, wgb[...]))


def _rope_tables(pos, half, group):
    inv_freq = ROPE_THETA ** (-jnp.arange(half, dtype=F32) / half)
    ang = pos.astype(F32)[:, None] * inv_freq[None, :]
    cos, sin = jnp.cos(ang), jnp.sin(ang)
    reps = LANES // group
    cos_t = jnp.tile(jnp.concatenate([cos, cos], axis=1), (1, reps))
    sin_t = jnp.tile(jnp.concatenate([-sin, sin], axis=1), (1, reps))
    return cos_t, sin_t


def _project(x, pos, pw, tm):
    t, d = x.shape
    cos_a, sin_a = _rope_tables(pos, HD_A // 2, HD_A)
    cos_b, sin_b = _rope_tables(pos, HD_B // 2, HD_B)
    row = lambda w: pl.BlockSpec((tm, w), lambda i: (i, 0))
    weights = [pw[k] for k in ("wqa", "wka", "wva", "wqb", "wkb", "wvb", "wqi", "wkw", "wga", "wgb")]
    consts = [pw[k] for k in ("gqa", "gka", "gqb", "gkb", "gki")]
    in_specs = ([row(d), _const_spec((1, d))] + [_const_spec(w.shape) for w in weights]
                + [_const_spec(c.shape) for c in consts] + [row(LANES)] * 4
                + [_const_spec((LANES, LANES))] * 2)
    outs = [(W_QA, BF16), (W_KA, F32), (W_KA, BF16), (W_VA, F32), (W_VA, BF16),
            (W_QB, BF16), (W_KB, F32), (W_KB, BF16), (W_KB, F32), (W_KB, BF16),
            (W_QI, BF16), (D_IDX, F32), (D_IDX, BF16), (LANES, F32), (d, F32), (d, F32)]
    return pl.pallas_call(
        _proj_kernel,
        out_shape=[jax.ShapeDtypeStruct((t, w), dt) for w, dt in outs],
        grid=(t // tm,),
        in_specs=in_specs,
        out_specs=[row(w) for w, _ in outs],
        compiler_params=_cparams(("parallel",)),
        name="proj",
    )(x, pw["g_attn"], *weights, *consts, cos_a, sin_a, cos_b, sin_b, pw["ones64"], pw["ones128"])


def _lane_rep(col):
    return jnp.broadcast_to(col, (col.shape[0], LANES))


def _online_softmax_step(s, v, m_ref, l_ref, acc_ref, idx):
    m_old = m_ref[idx]
    m_new = jnp.maximum(m_old, _lane_rep(jnp.max(s, axis=1, keepdims=True)))
    alpha = jnp.exp(m_old - m_new)
    p = jnp.exp(s - m_new[:, :1])
    l_ref[idx] = alpha * l_ref[idx] + _lane_rep(jnp.sum(p, axis=1, keepdims=True))
    acc_ref[idx] = alpha * acc_ref[idx] + _dot(p.astype(BF16), v)
    m_ref[idx] = m_new


def _split_components(qg):
    lane = lax.broadcasted_iota(I32, qg.shape, 1)
    zero = jnp.zeros_like(qg)
    return jnp.concatenate([jnp.where(lane < HD_A, qg, zero), jnp.where(lane >= HD_A, qg, zero)], axis=0)


def _lambda(lq1, lk1, lq2, lk2, lam_init):
    a = jnp.exp(jnp.sum(lq1[...] * lk1[...], axis=1, keepdims=True))
    b = jnp.exp(jnp.sum(lq2[...] * lk2[...], axis=1, keepdims=True))
    return a - b + lam_init


def _pdiff_kernel(lq1, lk1, lq2, lk2, q_ref, k_ref, v_ref, o_ref, lhs, m_s, l_s, acc, *, tq, lam_init):
    i = pl.program_id(0)
    for hg in range(H_A):
        lhs[hg] = _split_components(q_ref[:, hg * LANES:(hg + 1) * LANES])
    m_s[...] = jnp.full(m_s.shape, NEG_BIG, F32)
    l_s[...] = jnp.zeros(l_s.shape, F32)
    acc[...] = jnp.zeros(acc.shape, F32)

    def tile(j, masked):
        start = pl.multiple_of(j * tq, tq)
        for kv in range(KV_A):
            kt = k_ref[pl.ds(start, tq), kv * LANES:(kv + 1) * LANES]
            vt = v_ref[pl.ds(start, tq), kv * LANES:(kv + 1) * LANES]
            for g in range(G_A):
                hg = kv * G_A + g
                s = _dot_t(lhs[hg], kt)
                if masked:
                    r = lax.broadcasted_iota(I32, s.shape, 0) % tq
                    c = lax.broadcasted_iota(I32, s.shape, 1)
                    s = jnp.where(c <= r, s, -jnp.inf)
                _online_softmax_step(s, vt, m_s, l_s, acc, hg)

    def body(j, carry):
        tile(j, False)
        return carry

    lax.fori_loop(0, i, body, 0)
    tile(i, True)
    lam = _lambda(lq1, lk1, lq2, lk2, lam_init)
    for hg in range(H_A):
        o = acc[hg] / l_s[hg]
        o_ref[:, hg * LANES:(hg + 1) * LANES] = o[:tq] - lam * o[tq:]


def _prompt_diff_attn(lams, qa, ka, va, lam_init, tq):
    t = qa.shape[0]
    kern = functools.partial(_pdiff_kernel, tq=tq, lam_init=lam_init)
    return pl.pallas_call(
        kern,
        out_shape=jax.ShapeDtypeStruct((t, H_A * DV_A), F32),
        grid=(t // tq,),
        in_specs=[_const_spec((1, HD_A))] * 4 + [pl.BlockSpec((tq, W_QA), lambda i: (i, 0)),
                                                  _const_spec(ka.shape), _const_spec(va.shape)],
        out_specs=pl.BlockSpec((tq, H_A * DV_A), lambda i: (i, 0)),
        scratch_shapes=[pltpu.VMEM((H_A, 2 * tq, LANES), BF16), pltpu.VMEM((H_A, 2 * tq, LANES), F32),
                        pltpu.VMEM((H_A, 2 * tq, LANES), F32), pltpu.VMEM((H_A, 2 * tq, LANES), F32)],
        compiler_params=_cparams(("parallel",)),
        name="prompt_diff_attn",
    )(*lams, qa, ka, va)


def _sortable_key(score):
    bits = pltpu.bitcast(score, I32)
    return bits ^ ((bits >> 31) & 0x7FFFFFFF)


def _kth_largest_key(count_ge, rows, topk):
    zero = jnp.zeros((rows, 1), I32)
    t = jnp.where(count_ge(zero) >= topk, zero, jnp.full((rows, 1), INT_MIN, I32))

    def body(b, t):
        cand = t | jnp.left_shift(jnp.int32(1), 30 - b)
        return jnp.where(count_ge(cand) >= topk, cand, t)

    return lax.fori_loop(0, 31, body, t)


def _pdsa_kernel(qi_ref, wi_ref, qb_ref, ki_ref, kb_ref, vb_ref, o_ref, keys, m_s, l_s, acc, *, tq, tk, topk):
    i = pl.program_id(0)
    nt = (i * tq + tq + tk - 1) // tk
    row = i * tq + lax.broadcasted_iota(I32, (tq, tk), 0)
    col0 = lax.broadcasted_iota(I32, (tq, tk), 1)

    qi_st = jnp.concatenate([qi_ref[:, h * D_IDX:(h + 1) * D_IDX] for h in range(H_IDX)], axis=0)
    wv = wi_ref[...]
    w_cols = [wv[:, D_IDX + h:D_IDX + h + 1] for h in range(H_IDX)]

    def score_tile(j, carry):
        start = pl.multiple_of(j * tk, tk)
        lg = _dot_t(qi_st, ki_ref[pl.ds(start, tk), :])
        sc = jnp.zeros((tq, tk), F32)
        for h in range(H_IDX):
            sc = sc + jnp.maximum(lg[h * tq:(h + 1) * tq], 0.0) * w_cols[h]
        keys[j] = jnp.where(col0 + j * tk <= row, _sortable_key(sc), INT_MIN)
        return carry

    lax.fori_loop(0, nt, score_tile, 0)

    def count_ge(cand):
        def body(j, cnt):
            hit = (keys[j] >= cand).astype(I32)
            for c in range(tk // LANES):
                cnt = cnt + hit[:, c * LANES:(c + 1) * LANES]
            return cnt
        cnt = lax.fori_loop(0, nt, body, jnp.zeros((tq, LANES), I32))
        return jnp.sum(cnt, axis=1, keepdims=True)

    thr = _kth_largest_key(count_ge, tq, topk)

    m_s[...] = jnp.full(m_s.shape, NEG_BIG, F32)
    l_s[...] = jnp.zeros(l_s.shape, F32)
    acc[...] = jnp.zeros(acc.shape, F32)
    q_st = [jnp.concatenate([qb_ref[:, (kv * G_B + g) * HD_B:(kv * G_B + g + 1) * HD_B] for g in range(G_B)], axis=0)
            for kv in range(KV_B)]

    def attn_tile(j, carry):
        start = pl.multiple_of(j * tk, tk)
        kt = keys[j]
        sel = (kt >= thr) & (kt > INT_MIN)
        sel = jnp.concatenate([sel] * G_B, axis=0)
        for kv in range(KV_B):
            s = _dot_t(q_st[kv], kb_ref[pl.ds(start, tk), kv * HD_B:(kv + 1) * HD_B])
            s = jnp.where(sel, s, -jnp.inf)
            _online_softmax_step(s, vb_ref[pl.ds(start, tk), kv * HD_B:(kv + 1) * HD_B], m_s, l_s, acc, kv)
        return carry

    lax.fori_loop(0, nt, attn_tile, 0)
    for kv in range(KV_B):
        o = acc[kv] / l_s[kv]
        for g in range(G_B):
            hg = kv * G_B + g
            o_ref[:, hg * HD_B:(hg + 1) * HD_B] = o[g * tq:(g + 1) * tq]


def _prompt_dsa(qi, wi, qb, ki, kb, vb, tq, tk):
    t = qb.shape[0]
    topk = min(TOPK_MAX, t // 4)
    kern = functools.partial(_pdsa_kernel, tq=tq, tk=tk, topk=topk)
    row = lambda w: pl.BlockSpec((tq, w), lambda i: (i, 0))
    return pl.pallas_call(
        kern,
        out_shape=jax.ShapeDtypeStruct((t, H_B * HD_B), F32),
        grid=(t // tq,),
        in_specs=[row(W_QI), row(LANES), row(W_QB), _const_spec(ki.shape), _const_spec(kb.shape),
                  _const_spec(vb.shape)],
        out_specs=row(H_B * HD_B),
        scratch_shapes=[pltpu.VMEM((t // tk, tq, tk), I32), pltpu.VMEM((KV_B, G_B * tq, LANES), F32),
                        pltpu.VMEM((KV_B, G_B * tq, LANES), F32), pltpu.VMEM((KV_B, G_B * tq, LANES), F32)],
        compiler_params=_cparams(("parallel",)),
        name="prompt_dsa",
    )(qi, wi, qb, ki, kb, vb)


def _sdiff_kernel(pt_ref, lq1, lk1, lq2, lk2, q_ref, kn_ref, vn_ref, *rest, pps, page, tn, lam_init):
    kpages, vpages = rest[:pps], rest[pps:2 * pps]
    o_ref, kc, vc, lhs, m_s, l_s, acc = rest[2 * pps:]
    j = pl.program_id(1)

    @pl.when(j == 0)
    def _():
        q = q_ref[0].astype(F32)
        for kv in range(KV_A):
            parts = [_split_components(q[:, (kv * G_A + g) * LANES:(kv * G_A + g + 1) * LANES]) for g in range(G_A)]
            lhs[kv] = jnp.concatenate(parts, axis=0).astype(BF16)
        m_s[...] = jnp.full(m_s.shape, NEG_BIG, F32)
        l_s[...] = jnp.zeros(l_s.shape, F32)
        acc[...] = jnp.zeros(acc.shape, F32)

    for p in range(pps):
        kc[p * page:(p + 1) * page, :] = kpages[p][0].astype(BF16)
        vc[p * page:(p + 1) * page, :] = vpages[p][0].astype(BF16)
    for kv in range(KV_A):
        s = _dot_t(lhs[kv], kc[:, kv * LANES:(kv + 1) * LANES])
        _online_softmax_step(s, vc[:, kv * LANES:(kv + 1) * LANES], m_s, l_s, acc, kv)

    @pl.when(j == pl.num_programs(1) - 1)
    def _():
        kn = kn_ref[0]
        vn = vn_ref[0]
        lam = _lambda(lq1, lk1, lq2, lk2, lam_init)
        for kv in range(KV_A):
            s = _dot_t(lhs[kv], kn[:, kv * LANES:(kv + 1) * LANES])
            r = lax.broadcasted_iota(I32, s.shape, 0) % tn
            c = lax.broadcasted_iota(I32, s.shape, 1)
            s = jnp.where(c <= r, s, -jnp.inf)
            _online_softmax_step(s, vn[:, kv * LANES:(kv + 1) * LANES], m_s, l_s, acc, kv)
            o = acc[kv] / l_s[kv]
            for g in range(G_A):
                hg = kv * G_A + g
                o_ref[0, :, hg * LANES:(hg + 1) * LANES] = (o[(2 * g) * tn:(2 * g + 1) * tn]
                                                            - lam * o[(2 * g + 1) * tn:(2 * g + 2) * tn])


def _sample_diff_attn(page_table, lams, qa, ka_new, va_new, pool_k, pool_v, lam_init, pps):
    nb, tn, _ = qa.shape
    n_pages = page_table.shape[1]
    page = pool_k.shape[1]
    kern = functools.partial(_sdiff_kernel, pps=pps, page=page, tn=tn, lam_init=lam_init)
    seq = lambda w: pl.BlockSpec((1, tn, w), lambda b, j, pt: (b, 0, 0))
    cst = lambda shape: pl.BlockSpec(shape, lambda b, j, pt: (0,) * len(shape))

    def page_spec(p):
        return pl.BlockSpec((1, page, pool_k.shape[2]), lambda b, j, pt: (pt[b, j * pps + p], 0, 0))

    rows = 2 * G_A * tn
    return pl.pallas_call(
        kern,
        out_shape=jax.ShapeDtypeStruct((nb, tn, H_A * DV_A), F32),
        grid_spec=pltpu.PrefetchScalarGridSpec(
            num_scalar_prefetch=1,
            grid=(nb, n_pages // pps),
            in_specs=[cst((1, HD_A))] * 4 + [seq(W_QA), seq(W_KA), seq(W_VA)]
                     + [page_spec(p) for p in range(pps)] * 2,
            out_specs=seq(H_A * DV_A),
            scratch_shapes=[pltpu.VMEM((pps * page, W_KA), BF16), pltpu.VMEM((pps * page, W_VA), BF16),
                            pltpu.VMEM((KV_A, rows, LANES), BF16), pltpu.VMEM((KV_A, rows, LANES), F32),
                            pltpu.VMEM((KV_A, rows, LANES), F32), pltpu.VMEM((KV_A, rows, LANES), F32)]),
        compiler_params=_cparams(("parallel", "arbitrary")),
        name="sample_diff_attn",
    )(page_table, *lams, qa, ka_new, va_new, *([pool_k] * pps), *([pool_v] * pps))


def _ssel_kernel(pt_ref, qi_ref, wi_ref, kin_ref, *rest, pps, page, tn, topk):
    ipages = rest[:pps]
    sel_ref, keys = rest[pps:]
    j = pl.program_id(1)
    tk = pps * page
    q = qi_ref[0].astype(F32)
    qi_st = jnp.concatenate([q[:, h * D_IDX:(h + 1) * D_IDX] for h in range(H_IDX)], axis=0).astype(BF16)
    wv = wi_ref[0]
    w_cols = [wv[:, D_IDX + h:D_IDX + h + 1] for h in range(H_IDX)]

    def scores(kmat):
        lg = _dot_t(qi_st, kmat)
        sc = jnp.zeros((tn, kmat.shape[0]), F32)
        for h in range(H_IDX):
            sc = sc + jnp.maximum(lg[h * tn:(h + 1) * tn], 0.0) * w_cols[h]
        return _sortable_key(sc)

    kmat = jnp.concatenate([ipages[p][0] for p in range(pps)], axis=0).astype(BF16)
    keys[j] = scores(kmat)

    @pl.when(j == pl.num_programs(1) - 1)
    def _():
        nk = scores(kin_ref[0])
        r = lax.broadcasted_iota(I32, nk.shape, 0)
        c = lax.broadcasted_iota(I32, nk.shape, 1)
        nk = jnp.where(c <= r, nk, INT_MIN)
        nsteps = keys.shape[0]

        def count_ge(cand):
            def body(jj, cnt):
                hit = (keys[jj] >= cand).astype(I32)
                for cc in range(tk // LANES):
                    cnt = cnt + hit[:, cc * LANES:(cc + 1) * LANES]
                return cnt
            cnt = lax.fori_loop(0, nsteps, body, jnp.zeros((tn, LANES), I32))
            return (jnp.sum(cnt, axis=1, keepdims=True)
                    + jnp.sum((nk >= cand).astype(I32), axis=1, keepdims=True))

        thr = _kth_largest_key(count_ge, tn, topk)
        for jj in range(nsteps):
            kt = keys[jj]
            sel_ref[0, :, jj * tk:(jj + 1) * tk] = ((kt >= thr) & (kt > INT_MIN)).astype(F32)
        pad = sel_ref.shape[2] - nsteps * tk
        newsel = ((nk >= thr) & (nk > INT_MIN)).astype(F32)
        sel_ref[0, :, nsteps * tk:] = jnp.concatenate([newsel, jnp.zeros((tn, pad - tn), F32)], axis=1)


def _sample_select(page_table, qi, wi, ki_new, pool_i, pps):
    nb, tn, _ = qi.shape
    n_pages = page_table.shape[1]
    page = pool_i.shape[1]
    past = n_pages * page
    topk = min(TOPK_MAX, (past + tn) // 4)
    kern = functools.partial(_ssel_kernel, pps=pps, page=page, tn=tn, topk=topk)
    seq = lambda w: pl.BlockSpec((1, tn, w), lambda b, j, pt: (b, 0, 0))

    def page_spec(p):
        return pl.BlockSpec((1, page, D_IDX), lambda b, j, pt: (pt[b, j * pps + p], 0, 0))

    return pl.pallas_call(
        kern,
        out_shape=jax.ShapeDtypeStruct((nb, tn, past + LANES), F32),
        grid_spec=pltpu.PrefetchScalarGridSpec(
            num_scalar_prefetch=1,
            grid=(nb, n_pages // pps),
            in_specs=[seq(W_QI), seq(LANES), seq(D_IDX)] + [page_spec(p) for p in range(pps)],
            out_specs=seq(past + LANES),
            scratch_shapes=[pltpu.VMEM((n_pages // pps, tn, pps * page), I32)]),
        compiler_params=_cparams(("parallel", "arbitrary")),
        name="sample_select",
    )(page_table, qi, wi, ki_new, *([pool_i] * pps))


def _sdsa_kernel(pt_ref, q_ref, sel_ref, seln_ref, kn_ref, vn_ref, *rest, pps, page, tn):
    kpages, vpages = rest[:pps], rest[pps:2 * pps]
    o_ref, kc, vc, lhs, m_s, l_s, acc = rest[2 * pps:]
    j = pl.program_id(1)

    @pl.when(j == 0)
    def _():
        q = q_ref[0].astype(F32)
        for kv in range(KV_B):
            parts = [q[:, (kv * G_B + g) * HD_B:(kv * G_B + g + 1) * HD_B] for g in range(G_B)]
            lhs[kv] = jnp.concatenate(parts, axis=0).astype(BF16)
        m_s[...] = jnp.full(m_s.shape, NEG_BIG, F32)
        l_s[...] = jnp.zeros(l_s.shape, F32)
        acc[...] = jnp.zeros(acc.shape, F32)

    def step(sel, kmat, vmat):
        sel = jnp.concatenate([sel > 0.5] * G_B, axis=0)
        for kv in range(KV_B):
            s = _dot_t(lhs[kv], kmat[:, kv * HD_B:(kv + 1) * HD_B])
            s = jnp.where(sel, s, -jnp.inf)
            _online_softmax_step(s, vmat[:, kv * HD_B:(kv + 1) * HD_B], m_s, l_s, acc, kv)

    for p in range(pps):
        kc[p * page:(p + 1) * page, :] = kpages[p][0].astype(BF16)
        vc[p * page:(p + 1) * page, :] = vpages[p][0].astype(BF16)
    step(sel_ref[0], kc[...], vc[...])

    @pl.when(j == pl.num_programs(1) - 1)
    def _():
        step(seln_ref[0][:, :tn], kn_ref[0], vn_ref[0])
        for kv in range(KV_B):
            o = acc[kv] / l_s[kv]
            for g in range(G_B):
                hg = kv * G_B + g
                o_ref[0, :, hg * HD_B:(hg + 1) * HD_B] = o[g * tn:(g + 1) * tn]


def _sample_dsa(page_table, qb, sel, kb_new, vb_new, pool_k, pool_v, pps):
    nb, tn, _ = qb.shape
    n_pages = page_table.shape[1]
    page = pool_k.shape[1]
    tk = pps * page
    nsteps = n_pages // pps
    kern = functools.partial(_sdsa_kernel, pps=pps, page=page, tn=tn)
    seq = lambda w: pl.BlockSpec((1, tn, w), lambda b, j, pt: (b, 0, 0))

    def page_spec(p):
        return pl.BlockSpec((1, page, W_KB), lambda b, j, pt: (pt[b, j * pps + p], 0, 0))

    rows = G_B * tn
    return pl.pallas_call(
        kern,
        out_shape=jax.ShapeDtypeStruct((nb, tn, H_B * HD_B), F32),
        grid_spec=pltpu.PrefetchScalarGridSpec(
            num_scalar_prefetch=1,
            grid=(nb, nsteps),
            in_specs=[seq(W_QB), pl.BlockSpec((1, tn, tk), lambda b, j, pt: (b, 0, j)),
                      pl.BlockSpec((1, tn, LANES), lambda b, j, pt: (b, 0, (n_pages * page) // LANES)),
                      seq(W_KB), seq(W_KB)] + [page_spec(p) for p in range(pps)] * 2,
            out_specs=seq(H_B * HD_B),
            scratch_shapes=[pltpu.VMEM((tk, W_KB), BF16), pltpu.VMEM((tk, W_KB), BF16),
                            pltpu.VMEM((KV_B, rows, LANES), BF16), pltpu.VMEM((KV_B, rows, LANES), F32),
                            pltpu.VMEM((KV_B, rows, LANES), F32), pltpu.VMEM((KV_B, rows, LANES), F32)]),
        compiler_params=_cparams(("parallel", "arbitrary")),
        name="sample_dsa",
    )(page_table, qb, sel, sel, kb_new, vb_new, *([pool_k] * pps), *([pool_v] * pps))


def _finish_kernel(x_ref, oa_ref, ob_ref, ga_ref, gb_ref, sub_ref, gm_ref, wa, wb, wo, wu, wd, y_ref, *, lam_init):
    oa = oa_ref[...]
    sub = sub_ref[...]
    chunks = []
    for h in range(H_A):
        c = oa[:, h * DV_A:(h + 1) * DV_A]
        ms = jnp.mean(c * c, axis=-1, keepdims=True)
        chunks.append(((c * lax.rsqrt(ms + EPS)) * sub) * (1.0 - lam_init))
    oan = jnp.concatenate(chunks, axis=1).astype(BF16)
    pa = _dot(oan, wa[...])
    pb = _dot(ob_ref[...].astype(BF16), wb[...])
    mix = (ga_ref[...] * pa + gb_ref[...] * pb).astype(BF16)
    x1 = x_ref[...] + _dot(mix, wo[...])
    ms = jnp.mean(x1 * x1, axis=-1, keepdims=True)
    xn = ((x1 * lax.rsqrt(ms + EPS)) * gm_ref[...]).astype(BF16)
    hid = jnp.maximum(_dot(xn, wu[...]), 0.0)
    y_ref[...] = x1 + _dot((hid * hid).astype(BF16), wd[...])


def _finish(x, oa, ob, ga, gb, pw, lam_init, tm):
    t, d = x.shape
    row = pl.BlockSpec((tm, d), lambda i: (i, 0))
    weights = [pw[k] for k in ("w_a_proj", "w_b_proj", "w_out", "w_up", "w_down")]
    kern = functools.partial(_finish_kernel, lam_init=lam_init)
    return pl.pallas_call(
        kern,
        out_shape=jax.ShapeDtypeStruct((t, d), F32),
        grid=(t // tm,),
        in_specs=[row] * 5 + [_const_spec((1, DV_A)), _const_spec((1, d))] + [_const_spec(w.shape) for w in weights],
        out_specs=row,
        compiler_params=_cparams(("parallel",)),
        name="finish",
    )(x, oa, ob, ga, gb, pw["sub_norm_a"], pw["g_mlp"], *weights)


def _prep_weights(l, g_attn_norm, w_in, q_norm_a, k_norm_a, sub_norm_a, q_norm_b, k_norm_b, idx_k_norm,
                  w_a_proj, w_b_proj, w_out, g_mlp_norm, w_up, w_down):
    w = w_in[l].astype(BF16)
    widths = (W_QA, W_KA, W_VA, W_QB, W_KB, W_KB, W_QI, D_IDX, H_IDX, w.shape[0], w.shape[0])
    offs = np.concatenate([[0], np.cumsum(widths)])
    seg = [w[:, int(offs[i]):int(offs[i + 1])] for i in range(len(widths))]
    wkw = jnp.concatenate([seg[7], seg[8], jnp.zeros((w.shape[0], LANES - D_IDX - H_IDX), BF16)], axis=1)
    lane = np.arange(LANES)
    row2 = lambda v: v.astype(F32)[None, :]
    return {
        "g_attn": row2(g_attn_norm[l]),
        "wqa": seg[0], "wka": seg[1], "wva": seg[2], "wqb": seg[3], "wkb": seg[4], "wvb": seg[5],
        "wqi": seg[6], "wkw": wkw, "wga": seg[9], "wgb": seg[10],
        "gqa": row2(jnp.tile(q_norm_a[l], W_QA // HD_A) * (HD_A ** -0.5)),
        "gka": row2(jnp.tile(k_norm_a[l], W_KA // HD_A)),
        "gqb": row2(jnp.tile(q_norm_b[l], W_QB // HD_B) * (HD_B ** -0.5)),
        "gkb": row2(jnp.tile(k_norm_b[l], W_KB // HD_B)),
        "gki": row2(jnp.concatenate([idx_k_norm[l], jnp.zeros((LANES - D_IDX,), F32)])),
        "ones64": jnp.asarray((lane[:, None] // HD_A) == (lane[None, :] // HD_A), BF16),
        "ones128": jnp.ones((LANES, LANES), BF16),
        "sub_norm_a": row2(sub_norm_a[l]), "g_mlp": row2(g_mlp_norm[l]),
        "w_a_proj": w_a_proj[l].astype(BF16), "w_b_proj": w_b_proj[l].astype(BF16),
        "w_out": w_out[l].astype(BF16), "w_up": w_up[l].astype(BF16), "w_down": w_down[l].astype(BF16),
    }


def _tile(n, pref):
    return pref if n % pref == 0 else n


def kernel(x_prompt, x_sample, cache_a_k, cache_a_v, cache_b_k, cache_b_v, cache_b_idx_k, page_table, g_attn_norm, w_in, q_norm_a, k_norm_a, lam_q1, lam_k1, lam_q2, lam_k2, sub_norm_a, q_norm_b, k_norm_b, idx_k_norm, w_a_proj, w_b_proj, w_out, g_mlp_norm, w_up, w_down):
    depth = w_in.shape[0]
    nb_p, seq, d = x_prompt.shape
    nb_s, tn, _ = x_sample.shape
    n_pages = page_table.shape[1]
    n_phys, page = cache_a_k.shape[1], cache_a_k.shape[2]
    past = n_pages * page
    assert nb_p == 1
    pos_p = jnp.arange(seq)
    pos_s = past + (jnp.arange(nb_s * tn) % tn)
    pps = 8 if n_pages % 8 == 0 else 1

    xp = x_prompt.reshape(seq, d)
    xs = x_sample.reshape(nb_s * tn, d)
    outs_p = [[] for _ in range(5)]
    outs_s = [[] for _ in range(5)]
    for l in range(depth):
        lam_init = 0.8 - 0.6 * math.exp(-0.3 * l)
        pw = _prep_weights(l, g_attn_norm, w_in, q_norm_a, k_norm_a, sub_norm_a, q_norm_b, k_norm_b, idx_k_norm,
                           w_a_proj, w_b_proj, w_out, g_mlp_norm, w_up, w_down)
        lams = [v[l].astype(F32)[None, :] for v in (lam_q1, lam_k1, lam_q2, lam_k2)]

        (qa, kaf, kab, vaf, vab, qb, kbf, kbb, vbf, vbb, qi, kif, kib, wi, ga, gb) = _project(
            xp, pos_p, pw, _tile(seq, 256))
        oa = _prompt_diff_attn(lams, qa, kab, vab, lam_init, _tile(seq, 256))
        ob = _prompt_dsa(qi, wi, qb, kib, kbb, vbb, _tile(seq, 128), _tile(seq, 512))
        xp = _finish(xp, oa, ob, ga, gb, pw, lam_init, _tile(seq, 256))
        for lst, v in zip(outs_p, (kaf.reshape(1, seq, KV_A, 2, HD_A), vaf.reshape(1, seq, KV_A, DV_A),
                                   kbf.reshape(1, seq, KV_B, HD_B), vbf.reshape(1, seq, KV_B, HD_B),
                                   kif.reshape(1, seq, D_IDX))):
            lst.append(v)

        (qa, kaf, kab, vaf, vab, qb, kbf, kbb, vbf, vbb, qi, kif, kib, wi, ga, gb) = _project(
            xs, pos_s, pw, _tile(nb_s * tn, 256))
        seq3 = lambda a: a.reshape(nb_s, tn, a.shape[-1])
        oa = _sample_diff_attn(page_table, lams, seq3(qa), seq3(kab), seq3(vab),
                               cache_a_k[l].reshape(n_phys, page, W_KA), cache_a_v[l].reshape(n_phys, page, W_VA),
                               lam_init, pps)
        sel = _sample_select(page_table, seq3(qi), seq3(wi), seq3(kib), cache_b_idx_k[l], pps)
        ob = _sample_dsa(page_table, seq3(qb), sel, seq3(kbb), seq3(vbb),
                         cache_b_k[l].reshape(n_phys, page, W_KB), cache_b_v[l].reshape(n_phys, page, W_KB), pps)
        xs = _finish(xs, oa.reshape(nb_s * tn, -1), ob.reshape(nb_s * tn, -1), ga, gb, pw, lam_init,
                     _tile(nb_s * tn, 256))
        for lst, v in zip(outs_s, (kaf.reshape(nb_s, tn, KV_A, 2, HD_A), vaf.reshape(nb_s, tn, KV_A, DV_A),
                                   kbf.reshape(nb_s, tn, KV_B, HD_B), vbf.reshape(nb_s, tn, KV_B, HD_B),
                                   kif.reshape(nb_s, tn, D_IDX))):
            lst.append(v)

    return (xp.reshape(nb_p, seq, d), xs.reshape(nb_s, tn, d),
            *[jnp.stack(v) for v in outs_p], *[jnp.stack(v) for v in outs_s])
```

```python
import functools
import math

import jax
import jax.numpy as jnp
import numpy as np
from jax import lax
from jax.experimental import pallas as pl
from jax.experimental.pallas import tpu as pltpu

F32 = jnp.float32
BF16 = jnp.bfloat16
I32 = jnp.int32
I16 = jnp.int16

LANES = 128
VMEM_LIMIT = 56 * 1024 * 1024
EPS = 1e-6
ROPE_THETA = 10000.0
TOPK_MAX = 256
NEG_BIG = -1e30
INT_MIN = -(2 ** 31)
LOG2E = 1.4426950408889634
MAX_BOUNDED_SCORE = 60.0

H_A, KV_A, HD_A = 8, 2, 64
G_A = H_A // KV_A
DV_A = 2 * HD_A
H_B, KV_B, HD_B = 8, 2, 128
G_B = H_B // KV_B
H_IDX, D_IDX = 8, 64
W_QA = H_A * 2 * HD_A
W_KA = KV_A * 2 * HD_A
W_VA = KV_A * DV_A
W_QB = H_B * HD_B
W_KB = KV_B * HD_B
W_QI = H_IDX * D_IDX


def _cparams(sem):
    return pltpu.CompilerParams(dimension_semantics=sem, vmem_limit_bytes=VMEM_LIMIT)


def _const_spec(shape):
    nd = len(shape)
    return pl.BlockSpec(shape, lambda *_: (0,) * nd, pipeline_mode=pl.Buffered(1))


def _dot_t(a, b):
    return lax.dot_general(a, b, (((1,), (1,)), ((), ())), preferred_element_type=F32)


def _dot(a, b):
    return jnp.dot(a, b, preferred_element_type=F32)


def _group_rms(h, ones_ref, inv_group):
    outs = []
    for c in range(h.shape[1] // LANES):
        hs = h[:, c * LANES:(c + 1) * LANES]
        sq = hs * hs
        hi = sq.astype(BF16)
        lo = (sq - hi.astype(F32)).astype(BF16)
        ss = _dot(hi, ones_ref[...]) + _dot(lo, ones_ref[...])
        outs.append(hs * lax.rsqrt(ss * inv_group + EPS))
    return outs


def _rope_chunks(chunks, gain, cos, sin, half):
    lane = lax.broadcasted_iota(I32, chunks[0].shape, 1)
    first = (lane % (2 * half)) < half
    outs = []
    for c, y in enumerate(chunks):
        y = y * gain[:, c * LANES:(c + 1) * LANES]
        if half == LANES // 2:
            partner = pltpu.roll(y, half, 1)
        else:
            partner = jnp.where(first, pltpu.roll(y, LANES - half, 1), pltpu.roll(y, half, 1))
        outs.append(y * cos + partner * sin)
    return outs


def _proj_kernel(x_ref, g_ref, wqa, wka, wva, wqb, wkb, wvb, wqi, wkw, wga, wgb,
                 gqa, gka, gqb, gkb, gki, cos_a, sin_a, cos_b, sin_b, ones64, ones128,
                 qa_o, kaf_o, kab_o, vaf_o, vab_o, qb_o, kbf_o, kbb_o, vbf_o, vbb_o,
                 qi_o, kif_o, kib_o, wi_o, ga_o, gb_o):
    x = x_ref[...]
    ms = jnp.mean(x * x, axis=-1, keepdims=True)
    xn = ((x * lax.rsqrt(ms + EPS)) * g_ref[...]).astype(BF16)
    ca, sa, cb, sb = cos_a[...], sin_a[...], cos_b[...], sin_b[...]

    def cat(chunks):
        return chunks[0] if len(chunks) == 1 else jnp.concatenate(chunks, axis=1)

    h = _dot(xn, wqa[...])
    qa_o[...] = cat(_rope_chunks(_group_rms(h, ones64, 1.0 / HD_A), gqa[...], ca, sa, HD_A // 2)).astype(BF16)
    h = _dot(xn, wka[...])
    ka = cat(_rope_chunks(_group_rms(h, ones64, 1.0 / HD_A), gka[...], ca, sa, HD_A // 2))
    kaf_o[...] = ka
    kab_o[...] = ka.astype(BF16)
    h = _dot(xn, wva[...])
    vaf_o[...] = h
    vab_o[...] = h.astype(BF16)
    h = _dot(xn, wqb[...])
    qb_o[...] = cat(_rope_chunks(_group_rms(h, ones128, 1.0 / HD_B), gqb[...], cb, sb, HD_B // 2)).astype(BF16)
    h = _dot(xn, wkb[...])
    kb = cat(_rope_chunks(_group_rms(h, ones128, 1.0 / HD_B), gkb[...], cb, sb, HD_B // 2))
    kbf_o[...] = kb
    kbb_o[...] = kb.astype(BF16)
    h = _dot(xn, wvb[...])
    vbf_o[...] = h
    vbb_o[...] = h.astype(BF16)
    h = _dot(xn, wqi[...])
    chunks = [h[:, c * LANES:(c + 1) * LANES] for c in range(W_QI // LANES)]
    ones_gain = jnp.full((1, W_QI), D_IDX ** -0.5, F32)
    qi_o[...] = cat(_rope_chunks(chunks, ones_gain, ca, sa, D_IDX // 2)).astype(BF16)
    h = _dot(xn, wkw[...])
    ki = _rope_chunks(_group_rms(h, ones64, 1.0 / D_IDX), gki[...], ca, sa, D_IDX // 2)[0]
    kif_o[...] = ki[:, :D_IDX]
    kib_o[...] = ki[:, :D_IDX].astype(BF16)
    wi_o[...] = h * (H_IDX ** -0.5)
    ga_o[...] = jax.nn.sigmoid(_dot(xn, wga[...]))
    gb_o[...] = jax.nn.sigmoid(_dot(xn, wgb[...]))


def _rope_tables(pos, half, group):
    inv_freq = ROPE_THETA ** (-jnp.arange(half, dtype=F32) / half)
    ang = pos.astype(F32)[:, None] * inv_freq[None, :]
    cos, sin = jnp.cos(ang), jnp.sin(ang)
    reps = LANES // group
    cos_t = jnp.tile(jnp.concatenate([cos, cos], axis=1), (1, reps))
    sin_t = jnp.tile(jnp.concatenate([-sin, sin], axis=1), (1, reps))
    return cos_t, sin_t


def _project(x, pos, pw, tm):
    t, d = x.shape
    cos_a, sin_a = _rope_tables(pos, HD_A // 2, HD_A)
    cos_b, sin_b = _rope_tables(pos, HD_B // 2, HD_B)
    row = lambda w: pl.BlockSpec((tm, w), lambda i: (i, 0))
    weights = [pw[k] for k in ("wqa", "wka", "wva", "wqb", "wkb", "wvb", "wqi", "wkw", "wga", "wgb")]
    consts = [pw[k] for k in ("gqa", "gka", "gqb", "gkb", "gki")]
    in_specs = ([row(d), _const_spec((1, d))] + [_const_spec(w.shape) for w in weights]
                + [_const_spec(c.shape) for c in consts] + [row(LANES)] * 4
                + [_const_spec((LANES, LANES))] * 2)
    outs = [(W_QA, BF16), (W_KA, F32), (W_KA, BF16), (W_VA, F32), (W_VA, BF16),
            (W_QB, BF16), (W_KB, F32), (W_KB, BF16), (W_KB, F32), (W_KB, BF16),
            (W_QI, BF16), (D_IDX, F32), (D_IDX, BF16), (LANES, F32), (d, F32), (d, F32)]
    return pl.pallas_call(
        _proj_kernel,
        out_shape=[jax.ShapeDtypeStruct((t, w), dt) for w, dt in outs],
        grid=(t // tm,),
        in_specs=in_specs,
        out_specs=[row(w) for w, _ in outs],
        compiler_params=_cparams(("parallel",)),
        name="proj",
    )(x, pw["g_attn"], *weights, *consts, cos_a, sin_a, cos_b, sin_b, pw["ones64"], pw["ones128"])


def _lane_rep(col):
    return jnp.broadcast_to(col, (col.shape[0], LANES))


def _online_softmax_step(s, v, m_ref, l_ref, acc_ref, idx):
    m_old = m_ref[idx]
    m_new = jnp.maximum(m_old, _lane_rep(jnp.max(s, axis=1, keepdims=True)))
    alpha = jnp.exp2(m_old - m_new)
    p = jnp.exp2(s - m_new[:, :1])
    l_ref[idx] = alpha * l_ref[idx] + _lane_rep(jnp.sum(p, axis=1, keepdims=True))
    acc_ref[idx] = alpha * acc_ref[idx] + _dot(p.astype(BF16), v)
    m_ref[idx] = m_new


def _bounded_softmax_step(s, keep, v, l_ref, acc_ref, idx):
    p = jnp.exp2(s)
    if keep is not None:
        p = jnp.where(keep, p, 0.0)
    part = p[:, :LANES]
    for c in range(1, p.shape[1] // LANES):
        part = part + p[:, c * LANES:(c + 1) * LANES]
    l_ref[idx] = l_ref[idx] + part
    acc_ref[idx] = acc_ref[idx] + _dot(p.astype(BF16), v)


def _row_sum(l, online):
    return l[:, :1] if online else jnp.sum(l, axis=1, keepdims=True)


def _split_components(qg):
    lane = lax.broadcasted_iota(I32, qg.shape, 1)
    zero = jnp.zeros_like(qg)
    return jnp.concatenate([jnp.where(lane < HD_A, qg, zero), jnp.where(lane >= HD_A, qg, zero)], axis=0)


def _lambda(lq1, lk1, lq2, lk2, lam_init):
    a = jnp.exp(jnp.sum(lq1[...] * lk1[...], axis=1, keepdims=True))
    b = jnp.exp(jnp.sum(lq2[...] * lk2[...], axis=1, keepdims=True))
    return a - b + lam_init


def _pdiff_kernel(lq1, lk1, lq2, lk2, q_ref, k_ref, v_ref, o_ref, lhs, m_s, l_s, acc, *, tq, lam_init, online):
    i = pl.program_id(0)
    for hg in range(H_A):
        lhs[hg] = _split_components(q_ref[:, hg * LANES:(hg + 1) * LANES])
    m_s[...] = jnp.full(m_s.shape, NEG_BIG, F32)
    l_s[...] = jnp.zeros(l_s.shape, F32)
    acc[...] = jnp.zeros(acc.shape, F32)

    def tile(j, masked):
        start = pl.multiple_of(j * tq, tq)
        for kv in range(KV_A):
            kt = k_ref[pl.ds(start, tq), kv * LANES:(kv + 1) * LANES]
            vt = v_ref[pl.ds(start, tq), kv * LANES:(kv + 1) * LANES]
            for g in range(G_A):
                hg = kv * G_A + g
                s = _dot_t(lhs[hg], kt)
                keep = None
                if masked:
                    r = lax.broadcasted_iota(I32, s.shape, 0) % tq
                    c = lax.broadcasted_iota(I32, s.shape, 1)
                    keep = c <= r
                if online:
                    if masked:
                        s = jnp.where(keep, s, -jnp.inf)
                    _online_softmax_step(s, vt, m_s, l_s, acc, hg)
                else:
                    _bounded_softmax_step(s, keep, vt, l_s, acc, hg)

    def body(j, carry):
        tile(j, False)
        return carry

    lax.fori_loop(0, i, body, 0)
    tile(i, True)
    lam = _lambda(lq1, lk1, lq2, lk2, lam_init)
    for hg in range(H_A):
        o = acc[hg] / _row_sum(l_s[hg], online)
        o_ref[:, hg * LANES:(hg + 1) * LANES] = o[:tq] - lam * o[tq:]


def _prompt_diff_attn(lams, qa, ka, va, lam_init, tq, online):
    t = qa.shape[0]
    kern = functools.partial(_pdiff_kernel, tq=tq, lam_init=lam_init, online=online)
    return pl.pallas_call(
        kern,
        out_shape=jax.ShapeDtypeStruct((t, H_A * DV_A), F32),
        grid=(t // tq,),
        in_specs=[_const_spec((1, HD_A))] * 4 + [pl.BlockSpec((tq, W_QA), lambda i: (i, 0)),
                                                  _const_spec(ka.shape), _const_spec(va.shape)],
        out_specs=pl.BlockSpec((tq, H_A * DV_A), lambda i: (i, 0)),
        scratch_shapes=[pltpu.VMEM((H_A, 2 * tq, LANES), BF16), pltpu.VMEM((H_A, 2 * tq, LANES), F32),
                        pltpu.VMEM((H_A, 2 * tq, LANES), F32), pltpu.VMEM((H_A, 2 * tq, LANES), F32)],
        compiler_params=_cparams(("parallel",)),
        name="prompt_diff_attn",
    )(*lams, qa, ka, va)


def _sortable_key(score):
    bits = pltpu.bitcast(score, I32)
    return bits ^ ((bits >> 31) & 0x7FFFFFFF)


def _kth_largest_key(count_ge, rows, topk):
    zero = jnp.zeros((rows, 1), I32)
    t = jnp.where(count_ge(zero) >= topk, zero, jnp.full((rows, 1), INT_MIN, I32))

    def body(b, t):
        cand = t | jnp.left_shift(jnp.int32(1), 30 - b)
        return jnp.where(count_ge(cand) >= topk, cand, t)

    return lax.fori_loop(0, 31, body, t)


I16_MIN, I16_MAX = -(2 ** 15), 2 ** 15 - 1


def _kth_largest_half(halves, nt, rows, tk, topk):
    def count_ge(cand):
        c16 = cand.astype(I16)

        def body(j, cnt):
            return cnt + jnp.where(halves[j] >= c16, jnp.int16(1), jnp.int16(0))

        cnt = lax.fori_loop(0, nt, body, jnp.zeros((rows, tk), I16))
        return jnp.sum(cnt.astype(I32), axis=1, keepdims=True)

    zero = jnp.zeros((rows, 1), I32)
    t = jnp.where(count_ge(zero) >= topk, zero, jnp.full((rows, 1), I16_MIN, I32))

    def body(b, t):
        cand = t | jnp.left_shift(jnp.int32(1), 14 - b)
        return jnp.where(count_ge(cand) >= topk, cand, t)

    return lax.fori_loop(0, 15, body, t)


def _pdsa_kernel(qi_ref, wi_ref, qb_ref, ki_ref, kb_ref, vb_ref, o_ref, keys, hi16, lo16, m_s, l_s, acc,
                 *, tq, tk, topk, online):
    i = pl.program_id(0)
    nt = (i * tq + tq + tk - 1) // tk
    row = i * tq + lax.broadcasted_iota(I32, (tq, tk), 0)
    col0 = lax.broadcasted_iota(I32, (tq, tk), 1)

    qi_st = jnp.concatenate([qi_ref[:, h * D_IDX:(h + 1) * D_IDX] for h in range(H_IDX)], axis=0)
    wv = wi_ref[...]
    w_cols = [wv[:, D_IDX + h:D_IDX + h + 1] for h in range(H_IDX)]

    def score_tile(j, carry):
        start = pl.multiple_of(j * tk, tk)
        lg = _dot_t(qi_st, ki_ref[pl.ds(start, tk), :])
        sc = jnp.zeros((tq, tk), F32)
        for h in range(H_IDX):
            sc = sc + jnp.maximum(lg[h * tq:(h + 1) * tq], 0.0) * w_cols[h]
        key = jnp.where(col0 + j * tk <= row, _sortable_key(sc), INT_MIN)
        keys[j] = key
        hi16[j] = (key >> 16).astype(I16)
        return carry

    lax.fori_loop(0, nt, score_tile, 0)

    top = _kth_largest_half(hi16, nt, tq, tk, topk)

    def low_tile(j, carry):
        key = keys[j]
        hi = key >> 16
        lo = (key & 0xFFFF) + I16_MIN
        lo16[j] = jnp.where(hi == top, lo, jnp.where(hi > top, I16_MAX, I16_MIN)).astype(I16)
        return carry

    lax.fori_loop(0, nt, low_tile, 0)
    low = _kth_largest_half(lo16, nt, tq, tk, topk)
    thr = top * 65536 + (low - I16_MIN)

    m_s[...] = jnp.full(m_s.shape, NEG_BIG, F32)
    l_s[...] = jnp.zeros(l_s.shape, F32)
    acc[...] = jnp.zeros(acc.shape, F32)
    q_st = [jnp.concatenate([qb_ref[:, (kv * G_B + g) * HD_B:(kv * G_B + g + 1) * HD_B] for g in range(G_B)], axis=0)
            for kv in range(KV_B)]

    def attn_tile(j, carry):
        start = pl.multiple_of(j * tk, tk)
        kt = keys[j]
        sel = (kt >= thr) & (kt > INT_MIN)
        sel = jnp.concatenate([sel] * G_B, axis=0)
        for kv in range(KV_B):
            s = _dot_t(q_st[kv], kb_ref[pl.ds(start, tk), kv * HD_B:(kv + 1) * HD_B])
            vt = vb_ref[pl.ds(start, tk), kv * HD_B:(kv + 1) * HD_B]
            if online:
                _online_softmax_step(jnp.where(sel, s, -jnp.inf), vt, m_s, l_s, acc, kv)
            else:
                _bounded_softmax_step(s, sel, vt, l_s, acc, kv)
        return carry

    lax.fori_loop(0, nt, attn_tile, 0)
    for kv in range(KV_B):
        o = acc[kv] / _row_sum(l_s[kv], online)
        for g in range(G_B):
            hg = kv * G_B + g
            o_ref[:, hg * HD_B:(hg + 1) * HD_B] = o[g * tq:(g + 1) * tq]


def _prompt_dsa(qi, wi, qb, ki, kb, vb, tq, tk, online):
    t = qb.shape[0]
    topk = min(TOPK_MAX, t // 4)
    kern = functools.partial(_pdsa_kernel, tq=tq, tk=tk, topk=topk, online=online)
    row = lambda w: pl.BlockSpec((tq, w), lambda i: (i, 0))
    return pl.pallas_call(
        kern,
        out_shape=jax.ShapeDtypeStruct((t, H_B * HD_B), F32),
        grid=(t // tq,),
        in_specs=[row(W_QI), row(LANES), row(W_QB), _const_spec(ki.shape), _const_spec(kb.shape),
                  _const_spec(vb.shape)],
        out_specs=row(H_B * HD_B),
        scratch_shapes=[pltpu.VMEM((t // tk, tq, tk), I32), pltpu.VMEM((t // tk, tq, tk), I16),
                        pltpu.VMEM((t // tk, tq, tk), I16), pltpu.VMEM((KV_B, G_B * tq, LANES), F32),
                        pltpu.VMEM((KV_B, G_B * tq, LANES), F32), pltpu.VMEM((KV_B, G_B * tq, LANES), F32)],
        compiler_params=_cparams(("parallel",)),
        name="prompt_dsa",
    )(qi, wi, qb, ki, kb, vb)


def _sdiff_kernel(pt_ref, lq1, lk1, lq2, lk2, q_ref, kn_ref, vn_ref, *rest, pps, page, tn, lam_init):
    kpages, vpages = rest[:pps], rest[pps:2 * pps]
    o_ref, kc, vc, lhs, m_s, l_s, acc = rest[2 * pps:]
    j = pl.program_id(1)

    @pl.when(j == 0)
    def _():
        q = q_ref[0].astype(F32)
        for kv in range(KV_A):
            parts = [_split_components(q[:, (kv * G_A + g) * LANES:(kv * G_A + g + 1) * LANES]) for g in range(G_A)]
            lhs[kv] = jnp.concatenate(parts, axis=0).astype(BF16)
        m_s[...] = jnp.full(m_s.shape, NEG_BIG, F32)
        l_s[...] = jnp.zeros(l_s.shape, F32)
        acc[...] = jnp.zeros(acc.shape, F32)

    for p in range(pps):
        kp = kpages[p][0]
        for kv in range(KV_A):
            kc[kv, :, p * page:(p + 1) * page] = kp[kv].reshape(2 * HD_A, page).astype(BF16)
            vc[kv, p * page:(p + 1) * page, :] = vpages[p][0, pl.ds(kv, page, stride=KV_A), :].astype(BF16)
    for kv in range(KV_A):
        s = _dot(lhs[kv], kc[kv])
        _online_softmax_step(s, vc[kv], m_s, l_s, acc, kv)

    @pl.when(j == pl.num_programs(1) - 1)
    def _():
        kn = kn_ref[0]
        vn = vn_ref[0]
        lam = _lambda(lq1, lk1, lq2, lk2, lam_init)
        for kv in range(KV_A):
            s = _dot_t(lhs[kv], kn[:, kv * LANES:(kv + 1) * LANES])
            r = lax.broadcasted_iota(I32, s.shape, 0) % tn
            c = lax.broadcasted_iota(I32, s.shape, 1)
            s = jnp.where(c <= r, s, -jnp.inf)
            _online_softmax_step(s, vn[:, kv * LANES:(kv + 1) * LANES], m_s, l_s, acc, kv)
            o = acc[kv] / l_s[kv]
            for g in range(G_A):
                hg = kv * G_A + g
                o_ref[0, :, hg * LANES:(hg + 1) * LANES] = (o[(2 * g) * tn:(2 * g + 1) * tn]
                                                            - lam * o[(2 * g + 1) * tn:(2 * g + 2) * tn])


def _sample_diff_attn(page_table, lams, qa, ka_new, va_new, pool_kt, pool_v, lam_init, pps):
    nb, tn, _ = qa.shape
    n_pages = page_table.shape[1]
    page = pool_kt.shape[-1]
    kern = functools.partial(_sdiff_kernel, pps=pps, page=page, tn=tn, lam_init=lam_init)
    seq = lambda w: pl.BlockSpec((1, tn, w), lambda b, j, pt: (b, 0, 0))
    cst = lambda shape: pl.BlockSpec(shape, lambda b, j, pt: (0,) * len(shape))

    def kpage_spec(p):
        return pl.BlockSpec((1,) + pool_kt.shape[1:], lambda b, j, pt: (pt[b, j * pps + p], 0, 0, 0, 0))

    def vpage_spec(p):
        return pl.BlockSpec((1,) + pool_v.shape[1:], lambda b, j, pt: (pt[b, j * pps + p], 0, 0))

    rows = 2 * G_A * tn
    return pl.pallas_call(
        kern,
        out_shape=jax.ShapeDtypeStruct((nb, tn, H_A * DV_A), F32),
        grid_spec=pltpu.PrefetchScalarGridSpec(
            num_scalar_prefetch=1,
            grid=(nb, n_pages // pps),
            in_specs=[cst((1, HD_A))] * 4 + [seq(W_QA), seq(W_KA), seq(W_VA)]
                     + [kpage_spec(p) for p in range(pps)] + [vpage_spec(p) for p in range(pps)],
            out_specs=seq(H_A * DV_A),
            scratch_shapes=[pltpu.VMEM((KV_A, 2 * HD_A, pps * page), BF16),
                            pltpu.VMEM((KV_A, pps * page, DV_A), BF16),
                            pltpu.VMEM((KV_A, rows, LANES), BF16), pltpu.VMEM((KV_A, rows, LANES), F32),
                            pltpu.VMEM((KV_A, rows, LANES), F32), pltpu.VMEM((KV_A, rows, LANES), F32)]),
        compiler_params=_cparams(("parallel", "arbitrary")),
        name="sample_diff_attn",
    )(page_table, *lams, qa, ka_new, va_new, *([pool_kt] * pps), *([pool_v] * pps))


def _ssel_kernel(pt_ref, qi_ref, wi_ref, kin_ref, *rest, pps, page, tn, topk):
    ipages = rest[:pps]
    sel_ref, keys = rest[pps:]
    j = pl.program_id(1)
    tk = pps * page
    q = qi_ref[0].astype(F32)
    qi_st = jnp.concatenate([q[:, h * D_IDX:(h + 1) * D_IDX] for h in range(H_IDX)], axis=0).astype(BF16)
    wv = wi_ref[0]
    w_cols = [wv[:, D_IDX + h:D_IDX + h + 1] for h in range(H_IDX)]

    def scores(lg):
        sc = jnp.zeros((tn, lg.shape[1]), F32)
        for h in range(H_IDX):
            sc = sc + jnp.maximum(lg[h * tn:(h + 1) * tn], 0.0) * w_cols[h]
        return _sortable_key(sc)

    kt = jnp.concatenate([ipages[p][0] for p in range(pps)], axis=1).astype(BF16)
    keys[j] = scores(_dot(qi_st, kt))

    @pl.when(j == pl.num_programs(1) - 1)
    def _():
        nk = scores(_dot_t(qi_st, kin_ref[0]))
        r = lax.broadcasted_iota(I32, nk.shape, 0)
        c = lax.broadcasted_iota(I32, nk.shape, 1)
        nk = jnp.where(c <= r, nk, INT_MIN)
        nsteps = keys.shape[0]

        def count_ge(cand):
            def body(jj, cnt):
                hit = (keys[jj] >= cand).astype(I32)
                for cc in range(tk // LANES):
                    cnt = cnt + hit[:, cc * LANES:(cc + 1) * LANES]
                return cnt
            cnt = lax.fori_loop(0, nsteps, body, jnp.zeros((tn, LANES), I32))
            return (jnp.sum(cnt, axis=1, keepdims=True)
                    + jnp.sum((nk >= cand).astype(I32), axis=1, keepdims=True))

        thr = _kth_largest_key(count_ge, tn, topk)
        for jj in range(nsteps):
            kt = keys[jj]
            sel_ref[0, :, jj * tk:(jj + 1) * tk] = ((kt >= thr) & (kt > INT_MIN)).astype(F32)
        pad = sel_ref.shape[2] - nsteps * tk
        newsel = ((nk >= thr) & (nk > INT_MIN)).astype(F32)
        sel_ref[0, :, nsteps * tk:] = jnp.concatenate([newsel, jnp.zeros((tn, pad - tn), F32)], axis=1)


def _sample_select(page_table, qi, wi, ki_new, pool_it, pps):
    nb, tn, _ = qi.shape
    n_pages = page_table.shape[1]
    page = pool_it.shape[2]
    past = n_pages * page
    topk = min(TOPK_MAX, (past + tn) // 4)
    kern = functools.partial(_ssel_kernel, pps=pps, page=page, tn=tn, topk=topk)
    seq = lambda w: pl.BlockSpec((1, tn, w), lambda b, j, pt: (b, 0, 0))

    def page_spec(p):
        return pl.BlockSpec((1, D_IDX, page), lambda b, j, pt: (pt[b, j * pps + p], 0, 0))

    return pl.pallas_call(
        kern,
        out_shape=jax.ShapeDtypeStruct((nb, tn, past + LANES), F32),
        grid_spec=pltpu.PrefetchScalarGridSpec(
            num_scalar_prefetch=1,
            grid=(nb, n_pages // pps),
            in_specs=[seq(W_QI), seq(LANES), seq(D_IDX)] + [page_spec(p) for p in range(pps)],
            out_specs=seq(past + LANES),
            scratch_shapes=[pltpu.VMEM((n_pages // pps, tn, pps * page), I32)]),
        compiler_params=_cparams(("parallel", "arbitrary")),
        name="sample_select",
    )(page_table, qi, wi, ki_new, *([pool_it] * pps))


def _sdsa_kernel(pt_ref, q_ref, sel_ref, seln_ref, kn_ref, vn_ref, *rest, pps, page, tn):
    kpages, vpages = rest[:pps], rest[pps:2 * pps]
    o_ref, kc, vc, lhs, m_s, l_s, acc = rest[2 * pps:]
    j = pl.program_id(1)

    @pl.when(j == 0)
    def _():
        q = q_ref[0].astype(F32)
        for kv in range(KV_B):
            parts = [q[:, (kv * G_B + g) * HD_B:(kv * G_B + g + 1) * HD_B] for g in range(G_B)]
            lhs[kv] = jnp.concatenate(parts, axis=0).astype(BF16)
        m_s[...] = jnp.full(m_s.shape, NEG_BIG, F32)
        l_s[...] = jnp.zeros(l_s.shape, F32)
        acc[...] = jnp.zeros(acc.shape, F32)

    def step(sel, kmats, vmats):
        sel = jnp.concatenate([sel > 0.5] * G_B, axis=0)
        for kv in range(KV_B):
            s = jnp.where(sel, _dot_t(lhs[kv], kmats[kv]), -jnp.inf)
            _online_softmax_step(s, vmats[kv], m_s, l_s, acc, kv)

    for p in range(pps):
        for kv in range(KV_B):
            kc[kv, p * page:(p + 1) * page, :] = kpages[p][0, pl.ds(kv, page, stride=KV_B), :].astype(BF16)
            vc[kv, p * page:(p + 1) * page, :] = vpages[p][0, pl.ds(kv, page, stride=KV_B), :].astype(BF16)
    step(sel_ref[0], [kc[kv] for kv in range(KV_B)], [vc[kv] for kv in range(KV_B)])

    @pl.when(j == pl.num_programs(1) - 1)
    def _():
        kn, vn = kn_ref[0], vn_ref[0]
        step(seln_ref[0][:, :tn], [kn[:, kv * HD_B:(kv + 1) * HD_B] for kv in range(KV_B)],
             [vn[:, kv * HD_B:(kv + 1) * HD_B] for kv in range(KV_B)])
        for kv in range(KV_B):
            o = acc[kv] / l_s[kv]
            for g in range(G_B):
                hg = kv * G_B + g
                o_ref[0, :, hg * HD_B:(hg + 1) * HD_B] = o[g * tn:(g + 1) * tn]


def _sample_dsa(page_table, qb, sel, kb_new, vb_new, pool_k, pool_v, pps):
    nb, tn, _ = qb.shape
    n_pages = page_table.shape[1]
    page = pool_k.shape[1] // KV_B
    tk = pps * page
    nsteps = n_pages // pps
    kern = functools.partial(_sdsa_kernel, pps=pps, page=page, tn=tn)
    seq = lambda w: pl.BlockSpec((1, tn, w), lambda b, j, pt: (b, 0, 0))

    def page_spec(p):
        return pl.BlockSpec((1,) + pool_k.shape[1:], lambda b, j, pt: (pt[b, j * pps + p], 0, 0))

    rows = G_B * tn
    return pl.pallas_call(
        kern,
        out_shape=jax.ShapeDtypeStruct((nb, tn, H_B * HD_B), F32),
        grid_spec=pltpu.PrefetchScalarGridSpec(
            num_scalar_prefetch=1,
            grid=(nb, nsteps),
            in_specs=[seq(W_QB), pl.BlockSpec((1, tn, tk), lambda b, j, pt: (b, 0, j)),
                      pl.BlockSpec((1, tn, LANES), lambda b, j, pt: (b, 0, (n_pages * page) // LANES)),
                      seq(W_KB), seq(W_KB)] + [page_spec(p) for p in range(pps)] * 2,
            out_specs=seq(H_B * HD_B),
            scratch_shapes=[pltpu.VMEM((KV_B, tk, HD_B), BF16), pltpu.VMEM((KV_B, tk, HD_B), BF16),
                            pltpu.VMEM((KV_B, rows, LANES), BF16), pltpu.VMEM((KV_B, rows, LANES), F32),
                            pltpu.VMEM((KV_B, rows, LANES), F32), pltpu.VMEM((KV_B, rows, LANES), F32)]),
        compiler_params=_cparams(("parallel", "arbitrary")),
        name="sample_dsa",
    )(page_table, qb, sel, sel, kb_new, vb_new, *([pool_k] * pps), *([pool_v] * pps))


def _finish_kernel(x_ref, oa_ref, ob_ref, ga_ref, gb_ref, sub_ref, gm_ref, wa, wb, wo, wu, wd, y_ref, *, lam_init):
    oa = oa_ref[...]
    sub = sub_ref[...]
    chunks = []
    for h in range(H_A):
        c = oa[:, h * DV_A:(h + 1) * DV_A]
        ms = jnp.mean(c * c, axis=-1, keepdims=True)
        chunks.append(((c * lax.rsqrt(ms + EPS)) * sub) * (1.0 - lam_init))
    oan = jnp.concatenate(chunks, axis=1).astype(BF16)
    pa = _dot(oan, wa[...])
    pb = _dot(ob_ref[...].astype(BF16), wb[...])
    mix = (ga_ref[...] * pa + gb_ref[...] * pb).astype(BF16)
    x1 = x_ref[...] + _dot(mix, wo[...])
    ms = jnp.mean(x1 * x1, axis=-1, keepdims=True)
    xn = ((x1 * lax.rsqrt(ms + EPS)) * gm_ref[...]).astype(BF16)
    hid = jnp.maximum(_dot(xn, wu[...]), 0.0)
    y_ref[...] = x1 + _dot((hid * hid).astype(BF16), wd[...])


def _finish(x, oa, ob, ga, gb, pw, lam_init, tm):
    t, d = x.shape
    row = pl.BlockSpec((tm, d), lambda i: (i, 0))
    weights = [pw[k] for k in ("w_a_proj", "w_b_proj", "w_out", "w_up", "w_down")]
    kern = functools.partial(_finish_kernel, lam_init=lam_init)
    return pl.pallas_call(
        kern,
        out_shape=jax.ShapeDtypeStruct((t, d), F32),
        grid=(t // tm,),
        in_specs=[row] * 5 + [_const_spec((1, DV_A)), _const_spec((1, d))] + [_const_spec(w.shape) for w in weights],
        out_specs=row,
        compiler_params=_cparams(("parallel",)),
        name="finish",
    )(x, oa, ob, ga, gb, pw["sub_norm_a"], pw["g_mlp"], *weights)


def _prep_weights(l, g_attn_norm, w_in, q_norm_a, k_norm_a, sub_norm_a, q_norm_b, k_norm_b, idx_k_norm,
                  w_a_proj, w_b_proj, w_out, g_mlp_norm, w_up, w_down):
    w = w_in[l].astype(BF16)
    widths = (W_QA, W_KA, W_VA, W_QB, W_KB, W_KB, W_QI, D_IDX, H_IDX, w.shape[0], w.shape[0])
    offs = np.concatenate([[0], np.cumsum(widths)])
    seg = [w[:, int(offs[i]):int(offs[i + 1])] for i in range(len(widths))]
    wkw = jnp.concatenate([seg[7], seg[8], jnp.zeros((w.shape[0], LANES - D_IDX - H_IDX), BF16)], axis=1)
    lane = np.arange(LANES)
    row2 = lambda v: v.astype(F32)[None, :]
    return {
        "g_attn": row2(g_attn_norm[l]),
        "wqa": seg[0], "wka": seg[1], "wva": seg[2], "wqb": seg[3], "wkb": seg[4], "wvb": seg[5],
        "wqi": seg[6], "wkw": wkw, "wga": seg[9], "wgb": seg[10],
        "gqa": row2(jnp.tile(q_norm_a[l], W_QA // HD_A) * (HD_A ** -0.5 * LOG2E)),
        "gka": row2(jnp.tile(k_norm_a[l], W_KA // HD_A)),
        "gqb": row2(jnp.tile(q_norm_b[l], W_QB // HD_B) * (HD_B ** -0.5 * LOG2E)),
        "gkb": row2(jnp.tile(k_norm_b[l], W_KB // HD_B)),
        "gki": row2(jnp.concatenate([idx_k_norm[l], jnp.zeros((LANES - D_IDX,), F32)])),
        "ones64": jnp.asarray((lane[:, None] // HD_A) == (lane[None, :] // HD_A), BF16),
        "ones128": jnp.ones((LANES, LANES), BF16),
        "sub_norm_a": row2(sub_norm_a[l]), "g_mlp": row2(g_mlp_norm[l]),
        "w_a_proj": w_a_proj[l].astype(BF16), "w_b_proj": w_b_proj[l].astype(BF16),
        "w_out": w_out[l].astype(BF16), "w_up": w_up[l].astype(BF16), "w_down": w_down[l].astype(BF16),
    }


def _tile(n, pref):
    return pref if n % pref == 0 else n


def _tiles(seq, n_sample_rows, n_pages):
    pps = next(p for p in (32, 16, 8, 4, 2, 1) if n_pages % p == 0)
    return {"proj": _tile(seq, 256), "diff": _tile(seq, 256), "dsa_q": _tile(seq, 128), "dsa_k": _tile(seq, 512),
            "finish": _tile(seq, 256), "sample_rows": _tile(n_sample_rows, 256), "pps": pps}


def _score_bound(q_gain, k_gain, head_dim):
    return (head_dim ** 0.5) * jnp.max(jnp.abs(q_gain)) * jnp.max(jnp.abs(k_gain))


def kernel(x_prompt, x_sample, cache_a_k, cache_a_v, cache_b_k, cache_b_v, cache_b_idx_k, page_table, g_attn_norm, w_in, q_norm_a, k_norm_a, lam_q1, lam_k1, lam_q2, lam_k2, sub_norm_a, q_norm_b, k_norm_b, idx_k_norm, w_a_proj, w_b_proj, w_out, g_mlp_norm, w_up, w_down):
    depth = w_in.shape[0]
    nb_p, seq, d = x_prompt.shape
    nb_s, tn, _ = x_sample.shape
    n_pages = page_table.shape[1]
    n_phys, page = cache_a_k.shape[1], cache_a_k.shape[2]
    past = n_pages * page
    assert nb_p == 1
    pos_p = jnp.arange(seq)
    pos_s = past + (jnp.arange(nb_s * tn) % tn)
    tl = _tiles(seq, nb_s * tn, n_pages)
    pps = tl["pps"]

    xp = x_prompt.reshape(seq, d)
    xs = x_sample.reshape(nb_s * tn, d)
    outs_p = [[] for _ in range(5)]
    outs_s = [[] for _ in range(5)]
    for l in range(depth):
        lam_init = 0.8 - 0.6 * math.exp(-0.3 * l)
        pw = _prep_weights(l, g_attn_norm, w_in, q_norm_a, k_norm_a, sub_norm_a, q_norm_b, k_norm_b, idx_k_norm,
                           w_a_proj, w_b_proj, w_out, g_mlp_norm, w_up, w_down)
        lams = [v[l].astype(F32)[None, :] for v in (lam_q1, lam_k1, lam_q2, lam_k2)]

        (qa, kaf, kab, vaf, vab, qb, kbf, kbb, vbf, vbb, qi, kif, kib, wi, ga, gb) = _project(
            xp, pos_p, pw, tl["proj"])
        diff = lambda online: functools.partial(_prompt_diff_attn, lams, qa, kab, vab, lam_init, tl["diff"], online)
        oa = lax.cond(_score_bound(q_norm_a[l], k_norm_a[l], HD_A) <= MAX_BOUNDED_SCORE, diff(False), diff(True))
        dsa = lambda online: functools.partial(_prompt_dsa, qi, wi, qb, kib, kbb, vbb, tl["dsa_q"], tl["dsa_k"],
                                               online)
        ob = lax.cond(_score_bound(q_norm_b[l], k_norm_b[l], HD_B) <= MAX_BOUNDED_SCORE, dsa(False), dsa(True))
        xp = _finish(xp, oa, ob, ga, gb, pw, lam_init, tl["finish"])
        for lst, v in zip(outs_p, (kaf.reshape(1, seq, KV_A, 2, HD_A), vaf.reshape(1, seq, KV_A, DV_A),
                                   kbf.reshape(1, seq, KV_B, HD_B), vbf.reshape(1, seq, KV_B, HD_B),
                                   kif.reshape(1, seq, D_IDX))):
            lst.append(v)

        (qa, kaf, kab, vaf, vab, qb, kbf, kbb, vbf, vbb, qi, kif, kib, wi, ga, gb) = _project(
            xs, pos_s, pw, tl["sample_rows"])
        seq3 = lambda a: a.reshape(nb_s, tn, a.shape[-1])
        pool_akt = jnp.transpose(cache_a_k[l], (0, 2, 3, 4, 1))
        pool_it = jnp.transpose(cache_b_idx_k[l], (0, 2, 1))
        rows2 = lambda c: c[l].reshape(n_phys, page * c.shape[3], c.shape[4])
        oa = _sample_diff_attn(page_table, lams, seq3(qa), seq3(kab), seq3(vab), pool_akt, rows2(cache_a_v),
                               lam_init, pps)
        sel = _sample_select(page_table, seq3(qi), seq3(wi), seq3(kib), pool_it, pps)
        ob = _sample_dsa(page_table, seq3(qb), sel, seq3(kbb), seq3(vbb), rows2(cache_b_k), rows2(cache_b_v), pps)
        xs = _finish(xs, oa.reshape(nb_s * tn, -1), ob.reshape(nb_s * tn, -1), ga, gb, pw, lam_init,
                     tl["sample_rows"])
        for lst, v in zip(outs_s, (kaf.reshape(nb_s, tn, KV_A, 2, HD_A), vaf.reshape(nb_s, tn, KV_A, DV_A),
                                   kbf.reshape(nb_s, tn, KV_B, HD_B), vbf.reshape(nb_s, tn, KV_B, HD_B),
                                   kif.reshape(nb_s, tn, D_IDX))):
            lst.append(v)

    return (xp.reshape(nb_p, seq, d), xs.reshape(nb_s, tn, d),
            *[jnp.stack(v) for v in outs_p], *[jnp.stack(v) for v in outs_s])
```

```python
import functools
import math

import jax
import jax.numpy as jnp
import numpy as np
from jax import lax
from jax.experimental import pallas as pl
from jax.experimental.pallas import tpu as pltpu

F32 = jnp.float32
BF16 = jnp.bfloat16
I32 = jnp.int32
I16 = jnp.int16

LANES = 128
VMEM_LIMIT = 56 * 1024 * 1024
EPS = 1e-6
ROPE_THETA = 10000.0
TOPK_MAX = 256
NEG_BIG = -1e30
INT_MIN = -(2 ** 31)
LOG2E = 1.4426950408889634
MAX_BOUNDED_SCORE = 60.0

H_A, KV_A, HD_A = 8, 2, 64
G_A = H_A // KV_A
DV_A = 2 * HD_A
H_B, KV_B, HD_B = 8, 2, 128
G_B = H_B // KV_B
H_IDX, D_IDX = 8, 64
W_QA = H_A * 2 * HD_A
W_KA = KV_A * 2 * HD_A
W_VA = KV_A * DV_A
W_QB = H_B * HD_B
W_KB = KV_B * HD_B
W_QI = H_IDX * D_IDX


def _cparams(sem):
    return pltpu.CompilerParams(dimension_semantics=sem, vmem_limit_bytes=VMEM_LIMIT)


def _const_spec(shape):
    nd = len(shape)
    return pl.BlockSpec(shape, lambda *_: (0,) * nd, pipeline_mode=pl.Buffered(1))


def _dot_t(a, b):
    return lax.dot_general(a, b, (((1,), (1,)), ((), ())), preferred_element_type=F32)


def _dot(a, b):
    return jnp.dot(a, b, preferred_element_type=F32)


def _group_rms(h, ones_ref, inv_group):
    outs = []
    for c in range(h.shape[1] // LANES):
        hs = h[:, c * LANES:(c + 1) * LANES]
        sq = hs * hs
        hi = sq.astype(BF16)
        lo = (sq - hi.astype(F32)).astype(BF16)
        ss = _dot(hi, ones_ref[...]) + _dot(lo, ones_ref[...])
        outs.append(hs * lax.rsqrt(ss * inv_group + EPS))
    return outs


def _rope_chunks(chunks, gain, cos, sin, half):
    lane = lax.broadcasted_iota(I32, chunks[0].shape, 1)
    first = (lane % (2 * half)) < half
    outs = []
    for c, y in enumerate(chunks):
        y = y * gain[:, c * LANES:(c + 1) * LANES]
        if half == LANES // 2:
            partner = pltpu.roll(y, half, 1)
        else:
            partner = jnp.where(first, pltpu.roll(y, LANES - half, 1), pltpu.roll(y, half, 1))
        outs.append(y * cos + partner * sin)
    return outs


def _proj_kernel(x_ref, g_ref, wqa, wka, wva, wqb, wkb, wvb, wqi, wkw, wga, wgb,
                 gqa, gka, gqb, gkb, gki, cos_a, sin_a, cos_b, sin_b, ones64, ones128,
                 qa_o, kaf_o, kab_o, vaf_o, vab_o, qb_o, kbf_o, kbb_o, vbf_o, vbb_o,
                 qi_o, kif_o, kib_o, wi_o, ga_o, gb_o):
    x = x_ref[...]
    ms = jnp.mean(x * x, axis=-1, keepdims=True)
    xn = ((x * lax.rsqrt(ms + EPS)) * g_ref[...]).astype(BF16)
    ca, sa, cb, sb = cos_a[...], sin_a[...], cos_b[...], sin_b[...]

    def cat(chunks):
        return chunks[0] if len(chunks) == 1 else jnp.concatenate(chunks, axis=1)

    h = _dot(xn, wqa[...])
    qa_o[...] = cat(_rope_chunks(_group_rms(h, ones64, 1.0 / HD_A), gqa[...], ca, sa, HD_A // 2)).astype(BF16)
    h = _dot(xn, wka[...])
    ka = cat(_rope_chunks(_group_rms(h, ones64, 1.0 / HD_A), gka[...], ca, sa, HD_A // 2))
    kaf_o[...] = ka
    kab_o[...] = ka.astype(BF16)
    h = _dot(xn, wva[...])
    vaf_o[...] = h
    vab_o[...] = h.astype(BF16)
    h = _dot(xn, wqb[...])
    qb_o[...] = cat(_rope_chunks(_group_rms(h, ones128, 1.0 / HD_B), gqb[...], cb, sb, HD_B // 2)).astype(BF16)
    h = _dot(xn, wkb[...])
    kb = cat(_rope_chunks(_group_rms(h, ones128, 1.0 / HD_B), gkb[...], cb, sb, HD_B // 2))
    kbf_o[...] = kb
    kbb_o[...] = kb.astype(BF16)
    h = _dot(xn, wvb[...])
    vbf_o[...] = h
    vbb_o[...] = h.astype(BF16)
    h = _dot(xn, wqi[...])
    chunks = [h[:, c * LANES:(c + 1) * LANES] for c in range(W_QI // LANES)]
    ones_gain = jnp.full((1, W_QI), D_IDX ** -0.5, F32)
    qi_o[...] = cat(_rope_chunks(chunks, ones_gain, ca, sa, D_IDX // 2)).astype(BF16)
    h = _dot(xn, wkw[...])
    ki = _rope_chunks(_group_rms(h, ones64, 1.0 / D_IDX), gki[...], ca, sa, D_IDX // 2)[0]
    kif_o[...] = ki[:, :D_IDX]
    kib_o[...] = ki[:, :D_IDX].astype(BF16)
    wi_o[...] = h * (H_IDX ** -0.5)
    ga_o[...] = jax.nn.sigmoid(_dot(xn, wga[...]))
    gb_o[...] = jax.nn.sigmoid(_dot(xn, wgb[...]))


def _rope_tables(pos, half, group):
    inv_freq = ROPE_THETA ** (-jnp.arange(half, dtype=F32) / half)
    ang = pos.astype(F32)[:, None] * inv_freq[None, :]
    cos, sin = jnp.cos(ang), jnp.sin(ang)
    reps = LANES // group
    cos_t = jnp.tile(jnp.concatenate([cos, cos], axis=1), (1, reps))
    sin_t = jnp.tile(jnp.concatenate([-sin, sin], axis=1), (1, reps))
    return cos_t, sin_t


def _project(x, pos, pw, tm):
    t, d = x.shape
    cos_a, sin_a = _rope_tables(pos, HD_A // 2, HD_A)
    cos_b, sin_b = _rope_tables(pos, HD_B // 2, HD_B)
    row = lambda w: pl.BlockSpec((tm, w), lambda i: (i, 0))
    weights = [pw[k] for k in ("wqa", "wka", "wva", "wqb", "wkb", "wvb", "wqi", "wkw", "wga", "wgb")]
    consts = [pw[k] for k in ("gqa", "gka", "gqb", "gkb", "gki")]
    in_specs = ([row(d), _const_spec((1, d))] + [_const_spec(w.shape) for w in weights]
                + [_const_spec(c.shape) for c in consts] + [row(LANES)] * 4
                + [_const_spec((LANES, LANES))] * 2)
    outs = [(W_QA, BF16), (W_KA, F32), (W_KA, BF16), (W_VA, F32), (W_VA, BF16),
            (W_QB, BF16), (W_KB, F32), (W_KB, BF16), (W_KB, F32), (W_KB, BF16),
            (W_QI, BF16), (D_IDX, F32), (D_IDX, BF16), (LANES, F32), (d, F32), (d, F32)]
    return pl.pallas_call(
        _proj_kernel,
        out_shape=[jax.ShapeDtypeStruct((t, w), dt) for w, dt in outs],
        grid=(t // tm,),
        in_specs=in_specs,
        out_specs=[row(w) for w, _ in outs],
        compiler_params=_cparams(("parallel",)),
        name="proj",
    )(x, pw["g_attn"], *weights, *consts, cos_a, sin_a, cos_b, sin_b, pw["ones64"], pw["ones128"])


def _lane_rep(col):
    return jnp.broadcast_to(col, (col.shape[0], LANES))


def _online_softmax_step(s, v, m_ref, l_ref, acc_ref, idx):
    m_old = m_ref[idx]
    m_new = jnp.maximum(m_old, _lane_rep(jnp.max(s, axis=1, keepdims=True)))
    alpha = jnp.exp2(m_old - m_new)
    p = jnp.exp2(s - m_new[:, :1])
    l_ref[idx] = alpha * l_ref[idx] + _lane_rep(jnp.sum(p, axis=1, keepdims=True))
    acc_ref[idx] = alpha * acc_ref[idx] + _dot(p.astype(BF16), v)
    m_ref[idx] = m_new


def _bounded_softmax_step(s, keep, v, l_ref, acc_ref, idx):
    p = jnp.exp2(s)
    if keep is not None:
        p = jnp.where(keep, p, 0.0)
    part = p[:, :LANES]
    for c in range(1, p.shape[1] // LANES):
        part = part + p[:, c * LANES:(c + 1) * LANES]
    l_ref[idx] = l_ref[idx] + part
    acc_ref[idx] = acc_ref[idx] + _dot(p.astype(BF16), v)


def _row_sum(l, online):
    return l[:, :1] if online else jnp.sum(l, axis=1, keepdims=True)


def _split_components(qg):
    lane = lax.broadcasted_iota(I32, qg.shape, 1)
    zero = jnp.zeros_like(qg)
    return jnp.concatenate([jnp.where(lane < HD_A, qg, zero), jnp.where(lane >= HD_A, qg, zero)], axis=0)


def _lambda(lq1, lk1, lq2, lk2, lam_init):
    a = jnp.exp(jnp.sum(lq1[...] * lk1[...], axis=1, keepdims=True))
    b = jnp.exp(jnp.sum(lq2[...] * lk2[...], axis=1, keepdims=True))
    return a - b + lam_init


def _pdiff_kernel(lq1, lk1, lq2, lk2, q_ref, k_ref, v_ref, o_ref, lhs, m_s, l_s, acc, *, tq, lam_init, online):
    i = pl.program_id(0)
    for hg in range(H_A):
        lhs[hg] = _split_components(q_ref[:, hg * LANES:(hg + 1) * LANES])
    m_s[...] = jnp.full(m_s.shape, NEG_BIG, F32)
    l_s[...] = jnp.zeros(l_s.shape, F32)
    acc[...] = jnp.zeros(acc.shape, F32)

    def tile(j, masked):
        start = pl.multiple_of(j * tq, tq)
        for kv in range(KV_A):
            kt = k_ref[pl.ds(start, tq), kv * LANES:(kv + 1) * LANES]
            vt = v_ref[pl.ds(start, tq), kv * LANES:(kv + 1) * LANES]
            for g in range(G_A):
                hg = kv * G_A + g
                s = _dot_t(lhs[hg], kt)
                keep = None
                if masked:
                    r = lax.broadcasted_iota(I32, s.shape, 0) % tq
                    c = lax.broadcasted_iota(I32, s.shape, 1)
                    keep = c <= r
                if online:
                    if masked:
                        s = jnp.where(keep, s, -jnp.inf)
                    _online_softmax_step(s, vt, m_s, l_s, acc, hg)
                else:
                    _bounded_softmax_step(s, keep, vt, l_s, acc, hg)

    def body(j, carry):
        tile(j, False)
        return carry

    lax.fori_loop(0, i, body, 0)
    tile(i, True)
    lam = _lambda(lq1, lk1, lq2, lk2, lam_init)
    for hg in range(H_A):
        o = acc[hg] / _row_sum(l_s[hg], online)
        o_ref[:, hg * LANES:(hg + 1) * LANES] = o[:tq] - lam * o[tq:]


def _prompt_diff_attn(lams, qa, ka, va, lam_init, tq, online):
    t = qa.shape[0]
    kern = functools.partial(_pdiff_kernel, tq=tq, lam_init=lam_init, online=online)
    return pl.pallas_call(
        kern,
        out_shape=jax.ShapeDtypeStruct((t, H_A * DV_A), F32),
        grid=(t // tq,),
        in_specs=[_const_spec((1, HD_A))] * 4 + [pl.BlockSpec((tq, W_QA), lambda i: (i, 0)),
                                                  _const_spec(ka.shape), _const_spec(va.shape)],
        out_specs=pl.BlockSpec((tq, H_A * DV_A), lambda i: (i, 0)),
        scratch_shapes=[pltpu.VMEM((H_A, 2 * tq, LANES), BF16), pltpu.VMEM((H_A, 2 * tq, LANES), F32),
                        pltpu.VMEM((H_A, 2 * tq, LANES), F32), pltpu.VMEM((H_A, 2 * tq, LANES), F32)],
        compiler_params=_cparams(("parallel",)),
        name="prompt_diff_attn",
    )(*lams, qa, ka, va)


def _sortable_key(score):
    bits = pltpu.bitcast(score, I32)
    return bits ^ ((bits >> 31) & 0x7FFFFFFF)


I16_MIN, I16_MAX = -(2 ** 15), 2 ** 15 - 1


def _kth_largest_half(halves, nt, rows, tk, topk):
    width = min(tk, 2 * LANES)

    def count_ge(cand):
        c16 = cand.astype(I16)

        def body(j, cnt):
            hit = jnp.where(halves[j] >= c16, jnp.int16(1), jnp.int16(0))
            for c in range(tk // width):
                cnt = cnt + hit[:, c * width:(c + 1) * width]
            return cnt

        cnt = lax.fori_loop(0, nt, body, jnp.zeros((rows, width), I16))
        return jnp.sum(cnt.astype(I32), axis=1, keepdims=True)

    zero = jnp.zeros((rows, 1), I32)
    t = jnp.where(count_ge(zero) >= topk, zero, jnp.full((rows, 1), I16_MIN, I32))

    def body(b, t):
        cand = t | jnp.left_shift(jnp.int32(1), 14 - b)
        return jnp.where(count_ge(cand) >= topk, cand, t)

    return lax.fori_loop(0, 15, body, t)


def _topk_threshold(keys, hi16, lo16, nt, rows, tk, topk):
    top = _kth_largest_half(hi16, nt, rows, tk, topk)

    def low_tile(j, carry):
        key = keys[j]
        hi = key >> 16
        lo = (key & 0xFFFF) + I16_MIN
        lo16[j] = jnp.where(hi == top, lo, jnp.where(hi > top, I16_MAX, I16_MIN)).astype(I16)
        return carry

    lax.fori_loop(0, nt, low_tile, 0)
    low = _kth_largest_half(lo16, nt, rows, tk, topk)
    return jnp.maximum(top * 65536 + (low - I16_MIN), INT_MIN + 1)


def _pdsa_kernel(qi_ref, wi_ref, qb_ref, ki_ref, kb_ref, vb_ref, o_ref, keys, hi16, lo16, m_s, l_s, acc,
                 *, tq, tk, topk, online):
    i = pl.program_id(0)
    nt = (i * tq + tq + tk - 1) // tk
    row = i * tq + lax.broadcasted_iota(I32, (tq, tk), 0)
    col0 = lax.broadcasted_iota(I32, (tq, tk), 1)

    qi_st = jnp.concatenate([qi_ref[:, h * D_IDX:(h + 1) * D_IDX] for h in range(H_IDX)], axis=0)
    wv = wi_ref[...]
    w_cols = [wv[:, D_IDX + h:D_IDX + h + 1] for h in range(H_IDX)]

    def score_tile(j, carry):
        start = pl.multiple_of(j * tk, tk)
        lg = _dot_t(qi_st, ki_ref[pl.ds(start, tk), :])
        sc = jnp.zeros((tq, tk), F32)
        for h in range(H_IDX):
            sc = sc + jnp.maximum(lg[h * tq:(h + 1) * tq], 0.0) * w_cols[h]
        key = jnp.where(col0 + j * tk <= row, _sortable_key(sc), INT_MIN)
        keys[j] = key
        hi16[j] = (key >> 16).astype(I16)
        return carry

    lax.fori_loop(0, nt, score_tile, 0)

    thr = _topk_threshold(keys, hi16, lo16, nt, tq, tk, topk)

    m_s[...] = jnp.full(m_s.shape, NEG_BIG, F32)
    l_s[...] = jnp.zeros(l_s.shape, F32)
    acc[...] = jnp.zeros(acc.shape, F32)
    q_st = [jnp.concatenate([qb_ref[:, (kv * G_B + g) * HD_B:(kv * G_B + g + 1) * HD_B] for g in range(G_B)], axis=0)
            for kv in range(KV_B)]

    def attn_tile(j, carry):
        start = pl.multiple_of(j * tk, tk)
        kt = keys[j]
        sel = kt >= thr
        sel = jnp.concatenate([sel] * G_B, axis=0)
        for kv in range(KV_B):
            s = _dot_t(q_st[kv], kb_ref[pl.ds(start, tk), kv * HD_B:(kv + 1) * HD_B])
            vt = vb_ref[pl.ds(start, tk), kv * HD_B:(kv + 1) * HD_B]
            if online:
                _online_softmax_step(jnp.where(sel, s, -jnp.inf), vt, m_s, l_s, acc, kv)
            else:
                _bounded_softmax_step(s, sel, vt, l_s, acc, kv)
        return carry

    lax.fori_loop(0, nt, attn_tile, 0)
    for kv in range(KV_B):
        o = acc[kv] / _row_sum(l_s[kv], online)
        for g in range(G_B):
            hg = kv * G_B + g
            o_ref[:, hg * HD_B:(hg + 1) * HD_B] = o[g * tq:(g + 1) * tq]


def _prompt_dsa(qi, wi, qb, ki, kb, vb, tq, tk, online):
    t = qb.shape[0]
    topk = min(TOPK_MAX, t // 4)
    kern = functools.partial(_pdsa_kernel, tq=tq, tk=tk, topk=topk, online=online)
    row = lambda w: pl.BlockSpec((tq, w), lambda i: (i, 0))
    return pl.pallas_call(
        kern,
        out_shape=jax.ShapeDtypeStruct((t, H_B * HD_B), F32),
        grid=(t // tq,),
        in_specs=[row(W_QI), row(LANES), row(W_QB), _const_spec(ki.shape), _const_spec(kb.shape),
                  _const_spec(vb.shape)],
        out_specs=row(H_B * HD_B),
        scratch_shapes=[pltpu.VMEM((t // tk, tq, tk), I32), pltpu.VMEM((t // tk, tq, tk), I16),
                        pltpu.VMEM((t // tk, tq, tk), I16), pltpu.VMEM((KV_B, G_B * tq, LANES), F32),
                        pltpu.VMEM((KV_B, G_B * tq, LANES), F32), pltpu.VMEM((KV_B, G_B * tq, LANES), F32)],
        compiler_params=_cparams(("parallel",)),
        name="prompt_dsa",
    )(qi, wi, qb, ki, kb, vb)


def _fetch_pages(pt_ref, pools, bufs, sems, *, pps, nsteps, total):
    t = pl.program_id(0) * nsteps + pl.program_id(1)
    slot = t % 2

    def copies(tt, s, p):
        pg = pt_ref[tt // nsteps, (tt % nsteps) * pps + p]
        return [pltpu.make_async_copy(pool.at[pg], buf.at[s, p], sem.at[s])
                for pool, buf, sem in zip(pools, bufs, sems)]

    def start(tt, s):
        def body(p, carry):
            for cp in copies(tt, s, p):
                cp.start()
            return carry
        lax.fori_loop(0, pps, body, 0)

    @pl.when(t == 0)
    def _():
        start(t, slot)

    @pl.when(t + 1 < total)
    def _():
        start(t + 1, 1 - slot)

    def wait(p, carry):
        for cp in copies(t, slot, p):
            cp.wait()
        return carry

    lax.fori_loop(0, pps, wait, 0)
    return slot


def _page_scratch(pools, pps):
    bufs = [pltpu.VMEM((2, pps) + pool.shape[1:], pool.dtype) for pool in pools]
    sems = [pltpu.SemaphoreType.DMA((2,)) for _ in pools]
    return bufs + sems


def _sdiff_kernel(pt_ref, lq1, lk1, lq2, lk2, q_ref, kn_ref, vn_ref, kpool, vpool, o_ref,
                  kbuf, vbuf, ksem, vsem, kc, vc, lhs, m_s, l_s, acc, *, pps, page, tn, lam_init, nsteps, total):
    j = pl.program_id(1)

    @pl.when(j == 0)
    def _():
        q = q_ref[0].astype(F32)
        for kv in range(KV_A):
            parts = [_split_components(q[:, (kv * G_A + g) * LANES:(kv * G_A + g + 1) * LANES]) for g in range(G_A)]
            lhs[kv] = jnp.concatenate(parts, axis=0).astype(BF16)
        m_s[...] = jnp.full(m_s.shape, NEG_BIG, F32)
        l_s[...] = jnp.zeros(l_s.shape, F32)
        acc[...] = jnp.zeros(acc.shape, F32)

    slot = _fetch_pages(pt_ref, [kpool, vpool], [kbuf, vbuf], [ksem, vsem], pps=pps, nsteps=nsteps, total=total)
    for p in range(pps):
        for kv in range(KV_A):
            kc[kv, :, p * page:(p + 1) * page] = kbuf[slot, p, kv].reshape(2 * HD_A, page).astype(BF16)
            vc[kv, p * page:(p + 1) * page, :] = vbuf[slot, p, pl.ds(kv, page, stride=KV_A), :].astype(BF16)
    for kv in range(KV_A):
        s = _dot(lhs[kv], kc[kv])
        _online_softmax_step(s, vc[kv], m_s, l_s, acc, kv)

    @pl.when(j == pl.num_programs(1) - 1)
    def _():
        kn = kn_ref[0]
        vn = vn_ref[0]
        lam = _lambda(lq1, lk1, lq2, lk2, lam_init)
        for kv in range(KV_A):
            s = _dot_t(lhs[kv], kn[:, kv * LANES:(kv + 1) * LANES])
            r = lax.broadcasted_iota(I32, s.shape, 0) % tn
            c = lax.broadcasted_iota(I32, s.shape, 1)
            s = jnp.where(c <= r, s, -jnp.inf)
            _online_softmax_step(s, vn[:, kv * LANES:(kv + 1) * LANES], m_s, l_s, acc, kv)
            o = acc[kv] / l_s[kv]
            for g in range(G_A):
                hg = kv * G_A + g
                o_ref[0, :, hg * LANES:(hg + 1) * LANES] = (o[(2 * g) * tn:(2 * g + 1) * tn]
                                                            - lam * o[(2 * g + 1) * tn:(2 * g + 2) * tn])


def _sample_diff_attn(page_table, lams, qa, ka_new, va_new, pool_kt, pool_v, lam_init, pps):
    nb, tn, _ = qa.shape
    n_pages = page_table.shape[1]
    page = pool_kt.shape[-1]
    nsteps = n_pages // pps
    kern = functools.partial(_sdiff_kernel, pps=pps, page=page, tn=tn, lam_init=lam_init, nsteps=nsteps,
                             total=nb * nsteps)
    seq = lambda w: pl.BlockSpec((1, tn, w), lambda b, j, pt: (b, 0, 0))
    cst = lambda shape: pl.BlockSpec(shape, lambda b, j, pt: (0,) * len(shape))
    hbm = pl.BlockSpec(memory_space=pl.ANY)
    rows = 2 * G_A * tn
    return pl.pallas_call(
        kern,
        out_shape=jax.ShapeDtypeStruct((nb, tn, H_A * DV_A), F32),
        grid_spec=pltpu.PrefetchScalarGridSpec(
            num_scalar_prefetch=1,
            grid=(nb, nsteps),
            in_specs=[cst((1, HD_A))] * 4 + [seq(W_QA), seq(W_KA), seq(W_VA), hbm, hbm],
            out_specs=seq(H_A * DV_A),
            scratch_shapes=_page_scratch([pool_kt, pool_v], pps)
                           + [pltpu.VMEM((KV_A, 2 * HD_A, pps * page), BF16),
                              pltpu.VMEM((KV_A, pps * page, DV_A), BF16),
                              pltpu.VMEM((KV_A, rows, LANES), BF16), pltpu.VMEM((KV_A, rows, LANES), F32),
                              pltpu.VMEM((KV_A, rows, LANES), F32), pltpu.VMEM((KV_A, rows, LANES), F32)]),
        compiler_params=_cparams(("arbitrary", "arbitrary")),
        name="sample_diff_attn",
    )(page_table, *lams, qa, ka_new, va_new, pool_kt, pool_v)


def _sscore_kernel(pt_ref, qi_ref, wi_ref, kin_ref, ipool, keys_o, newkeys_o, ibuf, isem,
                   *, pps, page, tn, tile, nsteps, total):
    j = pl.program_id(1)
    q = qi_ref[0].astype(F32)
    qi_st = jnp.concatenate([q[:, h * D_IDX:(h + 1) * D_IDX] for h in range(H_IDX)], axis=0).astype(BF16)
    wv = wi_ref[0]
    w_cols = [wv[:, D_IDX + h:D_IDX + h + 1] for h in range(H_IDX)]

    def scores(lg):
        sc = jnp.zeros((tn, lg.shape[1]), F32)
        for h in range(H_IDX):
            sc = sc + jnp.maximum(lg[h * tn:(h + 1) * tn], 0.0) * w_cols[h]
        return _sortable_key(sc)

    slot = _fetch_pages(pt_ref, [ipool], [ibuf], [isem], pps=pps, nsteps=nsteps, total=total)
    kt = jnp.concatenate([ibuf[slot, p] for p in range(pps)], axis=1).astype(BF16)
    keys = scores(_dot(qi_st, kt))
    for c in range(pps * page // tile):
        keys_o[c] = keys[:, c * tile:(c + 1) * tile]

    @pl.when(j == nsteps - 1)
    def _():
        nk = scores(_dot_t(qi_st, kin_ref[0]))
        r = lax.broadcasted_iota(I32, nk.shape, 0)
        c = lax.broadcasted_iota(I32, nk.shape, 1)
        nk = jnp.where(c <= r, nk, INT_MIN)
        newkeys_o[...] = jnp.concatenate([nk, jnp.full((tn, LANES - tn), INT_MIN, I32)], axis=1)


def _sample_scores(page_table, qi, wi, ki_new, pool_it, pps, tile):
    nb, tn, _ = qi.shape
    n_pages = page_table.shape[1]
    page = pool_it.shape[2]
    nsteps = n_pages // pps
    tiles_per_step = pps * page // tile
    kern = functools.partial(_sscore_kernel, pps=pps, page=page, tn=tn, tile=tile, nsteps=nsteps,
                             total=nb * nsteps)
    seq = lambda w: pl.BlockSpec((1, tn, w), lambda b, j, pt: (b, 0, 0))
    return pl.pallas_call(
        kern,
        out_shape=[jax.ShapeDtypeStruct((n_pages * page // tile, nb * tn, tile), I32),
                   jax.ShapeDtypeStruct((nb * tn, LANES), I32)],
        grid_spec=pltpu.PrefetchScalarGridSpec(
            num_scalar_prefetch=1,
            grid=(nb, nsteps),
            in_specs=[seq(W_QI), seq(LANES), seq(D_IDX), pl.BlockSpec(memory_space=pl.ANY)],
            out_specs=[pl.BlockSpec((tiles_per_step, tn, tile), lambda b, j, pt: (j, b, 0)),
                       pl.BlockSpec((tn, LANES), lambda b, j, pt: (b, 0))],
            scratch_shapes=_page_scratch([pool_it], pps)),
        compiler_params=_cparams(("arbitrary", "arbitrary")),
        name="sample_scores",
    )(page_table, qi, wi, ki_new, pool_it)


def _sthr_kernel(keys_ref, newkeys_ref, sel_ref, keys, hi16, lo16, *, rows, tile, nt, topk):
    for j in range(nt - 1):
        keys[j] = keys_ref[j]
    pad = [jnp.full((rows, tile - LANES), INT_MIN, I32)] if tile > LANES else []
    keys[nt - 1] = jnp.concatenate([newkeys_ref[...]] + pad, axis=1)

    def split(j, carry):
        hi16[j] = (keys[j] >> 16).astype(I16)
        return carry

    lax.fori_loop(0, nt, split, 0)
    thr = _topk_threshold(keys, hi16, lo16, nt, rows, tile, topk)
    for j in range(nt - 1):
        sel_ref[:, j * tile:(j + 1) * tile] = (keys[j] >= thr).astype(F32)
    sel_ref[:, (nt - 1) * tile:] = (newkeys_ref[...] >= thr).astype(F32)


def _sample_select(keys, newkeys, topk):
    n_tiles, rows_all, tile = keys.shape
    rows = _tile(rows_all, 128)
    nt = n_tiles + 1
    kern = functools.partial(_sthr_kernel, rows=rows, tile=tile, nt=nt, topk=topk)
    width = n_tiles * tile + LANES
    return pl.pallas_call(
        kern,
        out_shape=jax.ShapeDtypeStruct((rows_all, width), F32),
        grid=(rows_all // rows,),
        in_specs=[pl.BlockSpec((n_tiles, rows, tile), lambda i: (0, i, 0)),
                  pl.BlockSpec((rows, LANES), lambda i: (i, 0))],
        out_specs=pl.BlockSpec((rows, width), lambda i: (i, 0)),
        scratch_shapes=[pltpu.VMEM((nt, rows, tile), I32), pltpu.VMEM((nt, rows, tile), I16),
                        pltpu.VMEM((nt, rows, tile), I16)],
        compiler_params=_cparams(("parallel",)),
        name="sample_select",
    )(keys, newkeys)


def _sdsa_kernel(pt_ref, q_ref, sel_ref, seln_ref, kn_ref, vn_ref, kpool, vpool, o_ref,
                 kbuf, vbuf, ksem, vsem, kc, vc, lhs, m_s, l_s, acc, *, pps, page, tn, nsteps, total):
    j = pl.program_id(1)

    @pl.when(j == 0)
    def _():
        q = q_ref[0].astype(F32)
        for kv in range(KV_B):
            parts = [q[:, (kv * G_B + g) * HD_B:(kv * G_B + g + 1) * HD_B] for g in range(G_B)]
            lhs[kv] = jnp.concatenate(parts, axis=0).astype(BF16)
        m_s[...] = jnp.full(m_s.shape, NEG_BIG, F32)
        l_s[...] = jnp.zeros(l_s.shape, F32)
        acc[...] = jnp.zeros(acc.shape, F32)

    def step(sel, kmats, vmats):
        sel = jnp.concatenate([sel > 0.5] * G_B, axis=0)
        for kv in range(KV_B):
            s = jnp.where(sel, _dot_t(lhs[kv], kmats[kv]), -jnp.inf)
            _online_softmax_step(s, vmats[kv], m_s, l_s, acc, kv)

    slot = _fetch_pages(pt_ref, [kpool, vpool], [kbuf, vbuf], [ksem, vsem], pps=pps, nsteps=nsteps, total=total)
    for p in range(pps):
        for kv in range(KV_B):
            kc[kv, p * page:(p + 1) * page, :] = kbuf[slot, p, pl.ds(kv, page, stride=KV_B), :].astype(BF16)
            vc[kv, p * page:(p + 1) * page, :] = vbuf[slot, p, pl.ds(kv, page, stride=KV_B), :].astype(BF16)
    step(sel_ref[0], [kc[kv] for kv in range(KV_B)], [vc[kv] for kv in range(KV_B)])

    @pl.when(j == nsteps - 1)
    def _():
        kn, vn = kn_ref[0], vn_ref[0]
        step(seln_ref[0][:, :tn], [kn[:, kv * HD_B:(kv + 1) * HD_B] for kv in range(KV_B)],
             [vn[:, kv * HD_B:(kv + 1) * HD_B] for kv in range(KV_B)])
        for kv in range(KV_B):
            o = acc[kv] / l_s[kv]
            for g in range(G_B):
                hg = kv * G_B + g
                o_ref[0, :, hg * HD_B:(hg + 1) * HD_B] = o[g * tn:(g + 1) * tn]


def _sample_dsa(page_table, qb, sel, kb_new, vb_new, pool_k, pool_v, pps):
    nb, tn, _ = qb.shape
    n_pages = page_table.shape[1]
    page = pool_k.shape[1] // KV_B
    tk = pps * page
    nsteps = n_pages // pps
    kern = functools.partial(_sdsa_kernel, pps=pps, page=page, tn=tn, nsteps=nsteps, total=nb * nsteps)
    seq = lambda w: pl.BlockSpec((1, tn, w), lambda b, j, pt: (b, 0, 0))
    hbm = pl.BlockSpec(memory_space=pl.ANY)
    rows = G_B * tn
    return pl.pallas_call(
        kern,
        out_shape=jax.ShapeDtypeStruct((nb, tn, H_B * HD_B), F32),
        grid_spec=pltpu.PrefetchScalarGridSpec(
            num_scalar_prefetch=1,
            grid=(nb, nsteps),
            in_specs=[seq(W_QB), pl.BlockSpec((1, tn, tk), lambda b, j, pt: (b, 0, j)),
                      pl.BlockSpec((1, tn, LANES), lambda b, j, pt: (b, 0, (n_pages * page) // LANES)),
                      seq(W_KB), seq(W_KB), hbm, hbm],
            out_specs=seq(H_B * HD_B),
            scratch_shapes=_page_scratch([pool_k, pool_v], pps)
                           + [pltpu.VMEM((KV_B, tk, HD_B), BF16), pltpu.VMEM((KV_B, tk, HD_B), BF16),
                              pltpu.VMEM((KV_B, rows, LANES), BF16), pltpu.VMEM((KV_B, rows, LANES), F32),
                              pltpu.VMEM((KV_B, rows, LANES), F32), pltpu.VMEM((KV_B, rows, LANES), F32)]),
        compiler_params=_cparams(("arbitrary", "arbitrary")),
        name="sample_dsa",
    )(page_table, qb, sel, sel, kb_new, vb_new, pool_k, pool_v)


def _finish_kernel(x_ref, oa_ref, ob_ref, ga_ref, gb_ref, sub_ref, gm_ref, wa, wb, wo, wu, wd, y_ref, *, lam_init):
    oa = oa_ref[...]
    sub = sub_ref[...]
    chunks = []
    for h in range(H_A):
        c = oa[:, h * DV_A:(h + 1) * DV_A]
        ms = jnp.mean(c * c, axis=-1, keepdims=True)
        chunks.append(((c * lax.rsqrt(ms + EPS)) * sub) * (1.0 - lam_init))
    oan = jnp.concatenate(chunks, axis=1).astype(BF16)
    pa = _dot(oan, wa[...])
    pb = _dot(ob_ref[...].astype(BF16), wb[...])
    mix = (ga_ref[...] * pa + gb_ref[...] * pb).astype(BF16)
    x1 = x_ref[...] + _dot(mix, wo[...])
    ms = jnp.mean(x1 * x1, axis=-1, keepdims=True)
    xn = ((x1 * lax.rsqrt(ms + EPS)) * gm_ref[...]).astype(BF16)
    hid = jnp.maximum(_dot(xn, wu[...]), 0.0)
    y_ref[...] = x1 + _dot((hid * hid).astype(BF16), wd[...])


def _finish(x, oa, ob, ga, gb, pw, lam_init, tm):
    t, d = x.shape
    row = pl.BlockSpec((tm, d), lambda i: (i, 0))
    weights = [pw[k] for k in ("w_a_proj", "w_b_proj", "w_out", "w_up", "w_down")]
    kern = functools.partial(_finish_kernel, lam_init=lam_init)
    return pl.pallas_call(
        kern,
        out_shape=jax.ShapeDtypeStruct((t, d), F32),
        grid=(t // tm,),
        in_specs=[row] * 5 + [_const_spec((1, DV_A)), _const_spec((1, d))] + [_const_spec(w.shape) for w in weights],
        out_specs=row,
        compiler_params=_cparams(("parallel",)),
        name="finish",
    )(x, oa, ob, ga, gb, pw["sub_norm_a"], pw["g_mlp"], *weights)


def _prep_weights(l, g_attn_norm, w_in, q_norm_a, k_norm_a, sub_norm_a, q_norm_b, k_norm_b, idx_k_norm,
                  w_a_proj, w_b_proj, w_out, g_mlp_norm, w_up, w_down):
    w = w_in[l].astype(BF16)
    widths = (W_QA, W_KA, W_VA, W_QB, W_KB, W_KB, W_QI, D_IDX, H_IDX, w.shape[0], w.shape[0])
    offs = np.concatenate([[0], np.cumsum(widths)])
    seg = [w[:, int(offs[i]):int(offs[i + 1])] for i in range(len(widths))]
    wkw = jnp.concatenate([seg[7], seg[8], jnp.zeros((w.shape[0], LANES - D_IDX - H_IDX), BF16)], axis=1)
    lane = np.arange(LANES)
    row2 = lambda v: v.astype(F32)[None, :]
    return {
        "g_attn": row2(g_attn_norm[l]),
        "wqa": seg[0], "wka": seg[1], "wva": seg[2], "wqb": seg[3], "wkb": seg[4], "wvb": seg[5],
        "wqi": seg[6], "wkw": wkw, "wga": seg[9], "wgb": seg[10],
        "gqa": row2(jnp.tile(q_norm_a[l], W_QA // HD_A) * (HD_A ** -0.5 * LOG2E)),
        "gka": row2(jnp.tile(k_norm_a[l], W_KA // HD_A)),
        "gqb": row2(jnp.tile(q_norm_b[l], W_QB // HD_B) * (HD_B ** -0.5 * LOG2E)),
        "gkb": row2(jnp.tile(k_norm_b[l], W_KB // HD_B)),
        "gki": row2(jnp.concatenate([idx_k_norm[l], jnp.zeros((LANES - D_IDX,), F32)])),
        "ones64": jnp.asarray((lane[:, None] // HD_A) == (lane[None, :] // HD_A), BF16),
        "ones128": jnp.ones((LANES, LANES), BF16),
        "sub_norm_a": row2(sub_norm_a[l]), "g_mlp": row2(g_mlp_norm[l]),
        "w_a_proj": w_a_proj[l].astype(BF16), "w_b_proj": w_b_proj[l].astype(BF16),
        "w_out": w_out[l].astype(BF16), "w_up": w_up[l].astype(BF16), "w_down": w_down[l].astype(BF16),
    }


def _tile(n, pref):
    return pref if n % pref == 0 else n


def _tiles(seq, n_sample_rows, n_pages):
    pps = next(p for p in (32, 16, 8, 4, 2, 1) if n_pages % p == 0)
    return {"proj": _tile(seq, 256), "diff": _tile(seq, 256), "dsa_q": _tile(seq, 128), "dsa_k": _tile(seq, 512),
            "finish": _tile(seq, 256), "sample_rows": _tile(n_sample_rows, 256), "pps": pps,
            "sel_tile": _tile(pps * 128, 512)}


def _score_bound(q_gain, k_gain, head_dim):
    return (head_dim ** 0.5) * jnp.max(jnp.abs(q_gain)) * jnp.max(jnp.abs(k_gain))


def kernel(x_prompt, x_sample, cache_a_k, cache_a_v, cache_b_k, cache_b_v, cache_b_idx_k, page_table, g_attn_norm, w_in, q_norm_a, k_norm_a, lam_q1, lam_k1, lam_q2, lam_k2, sub_norm_a, q_norm_b, k_norm_b, idx_k_norm, w_a_proj, w_b_proj, w_out, g_mlp_norm, w_up, w_down):
    depth = w_in.shape[0]
    nb_p, seq, d = x_prompt.shape
    nb_s, tn, _ = x_sample.shape
    n_pages = page_table.shape[1]
    n_phys, page = cache_a_k.shape[1], cache_a_k.shape[2]
    past = n_pages * page
    assert nb_p == 1
    pos_p = jnp.arange(seq)
    pos_s = past + (jnp.arange(nb_s * tn) % tn)
    tl = _tiles(seq, nb_s * tn, n_pages)
    pps = tl["pps"]

    xp = x_prompt.reshape(seq, d)
    xs = x_sample.reshape(nb_s * tn, d)
    outs_p = [[] for _ in range(5)]
    outs_s = [[] for _ in range(5)]
    for l in range(depth):
        lam_init = 0.8 - 0.6 * math.exp(-0.3 * l)
        pw = _prep_weights(l, g_attn_norm, w_in, q_norm_a, k_norm_a, sub_norm_a, q_norm_b, k_norm_b, idx_k_norm,
                           w_a_proj, w_b_proj, w_out, g_mlp_norm, w_up, w_down)
        lams = [v[l].astype(F32)[None, :] for v in (lam_q1, lam_k1, lam_q2, lam_k2)]

        (qa, kaf, kab, vaf, vab, qb, kbf, kbb, vbf, vbb, qi, kif, kib, wi, ga, gb) = _project(
            xs, pos_s, pw, tl["sample_rows"])
        seq3 = lambda a: a.reshape(nb_s, tn, a.shape[-1])
        pool_akt = jnp.transpose(cache_a_k[l], (0, 2, 3, 4, 1))
        pool_it = jnp.transpose(cache_b_idx_k[l], (0, 2, 1))
        rows2 = lambda c: c[l].reshape(n_phys, page * c.shape[3], c.shape[4])
        oa = _sample_diff_attn(page_table, lams, seq3(qa), seq3(kab), seq3(vab), pool_akt, rows2(cache_a_v),
                               lam_init, pps)
        keys, newkeys = _sample_scores(page_table, seq3(qi), seq3(wi), seq3(kib), pool_it, pps, tl["sel_tile"])
        sel = seq3(_sample_select(keys, newkeys, min(TOPK_MAX, (past + tn) // 4)))
        ob = _sample_dsa(page_table, seq3(qb), sel, seq3(kbb), seq3(vbb), rows2(cache_b_k), rows2(cache_b_v), pps)
        xs = _finish(xs, oa.reshape(nb_s * tn, -1), ob.reshape(nb_s * tn, -1), ga, gb, pw, lam_init,
                     tl["sample_rows"])
        for lst, v in zip(outs_s, (kaf.reshape(nb_s, tn, KV_A, 2, HD_A), vaf.reshape(nb_s, tn, KV_A, DV_A),
                                   kbf.reshape(nb_s, tn, KV_B, HD_B), vbf.reshape(nb_s, tn, KV_B, HD_B),
                                   kif.reshape(nb_s, tn, D_IDX))):
            lst.append(v)

        (qa, kaf, kab, vaf, vab, qb, kbf, kbb, vbf, vbb, qi, kif, kib, wi, ga, gb) = _project(
            xp, pos_p, pw, tl["proj"])
        diff = lambda online: functools.partial(_prompt_diff_attn, lams, qa, kab, vab, lam_init, tl["diff"], online)
        oa = lax.cond(_score_bound(q_norm_a[l], k_norm_a[l], HD_A) <= MAX_BOUNDED_SCORE, diff(False), diff(True))
        dsa = lambda online: functools.partial(_prompt_dsa, qi, wi, qb, kib, kbb, vbb, tl["dsa_q"], tl["dsa_k"],
                                               online)
        ob = lax.cond(_score_bound(q_norm_b[l], k_norm_b[l], HD_B) <= MAX_BOUNDED_SCORE, dsa(False), dsa(True))
        xp = _finish(xp, oa, ob, ga, gb, pw, lam_init, tl["finish"])
        for lst, v in zip(outs_p, (kaf.reshape(1, seq, KV_A, 2, HD_A), vaf.reshape(1, seq, KV_A, DV_A),
                                   kbf.reshape(1, seq, KV_B, HD_B), vbf.reshape(1, seq, KV_B, HD_B),
                                   kif.reshape(1, seq, D_IDX))):
            lst.append(v)

    return (xp.reshape(nb_p, seq, d), xs.reshape(nb_s, tn, d),
            *[jnp.stack(v) for v in outs_p], *[jnp.stack(v) for v in outs_s])
```

```python
import functools
import math

import jax
import jax.numpy as jnp
import numpy as np
from jax import lax
from jax.experimental import pallas as pl
from jax.experimental.pallas import tpu as pltpu

F32 = jnp.float32
BF16 = jnp.bfloat16
I32 = jnp.int32
I16 = jnp.int16

LANES = 128
VMEM_LIMIT = 56 * 1024 * 1024
EPS = 1e-6
ROPE_THETA = 10000.0
TOPK_MAX = 256
NEG_BIG = -1e30
INT_MIN = -(2 ** 31)
LOG2E = 1.4426950408889634
MAX_BOUNDED_SCORE = 60.0

H_A, KV_A, HD_A = 8, 2, 64
G_A = H_A // KV_A
DV_A = 2 * HD_A
H_B, KV_B, HD_B = 8, 2, 128
G_B = H_B // KV_B
H_IDX, D_IDX = 8, 64
W_QA = H_A * 2 * HD_A
W_KA = KV_A * 2 * HD_A
W_VA = KV_A * DV_A
W_QB = H_B * HD_B
W_KB = KV_B * HD_B
W_QI = H_IDX * D_IDX


def _cparams(sem):
    return pltpu.CompilerParams(dimension_semantics=sem, vmem_limit_bytes=VMEM_LIMIT)


def _const_spec(shape):
    nd = len(shape)
    return pl.BlockSpec(shape, lambda *_: (0,) * nd, pipeline_mode=pl.Buffered(1))


def _dot_t(a, b):
    return lax.dot_general(a, b, (((1,), (1,)), ((), ())), preferred_element_type=F32)


def _dot(a, b):
    return jnp.dot(a, b, preferred_element_type=F32)


def _group_rms(h, ones_ref, inv_group):
    outs = []
    for c in range(h.shape[1] // LANES):
        hs = h[:, c * LANES:(c + 1) * LANES]
        sq = hs * hs
        hi = sq.astype(BF16)
        lo = (sq - hi.astype(F32)).astype(BF16)
        ss = _dot(hi, ones_ref[...]) + _dot(lo, ones_ref[...])
        outs.append(hs * lax.rsqrt(ss * inv_group + EPS))
    return outs


def _rope_chunks(chunks, gain, cos, sin, half):
    lane = lax.broadcasted_iota(I32, chunks[0].shape, 1)
    first = (lane % (2 * half)) < half
    outs = []
    for c, y in enumerate(chunks):
        y = y * gain[:, c * LANES:(c + 1) * LANES]
        if half == LANES // 2:
            partner = pltpu.roll(y, half, 1)
        else:
            partner = jnp.where(first, pltpu.roll(y, LANES - half, 1), pltpu.roll(y, half, 1))
        outs.append(y * cos + partner * sin)
    return outs


def _proj_kernel(x_ref, g_ref, wqa, wka, wva, wqb, wkb, wvb, wqi, wkw, wga, wgb,
                 gqa, gka, gqb, gkb, gki, cos_a, sin_a, cos_b, sin_b, ones64, ones128,
                 qa_o, kaf_o, kab_o, vaf_o, vab_o, qb_o, kbf_o, kbb_o, vbf_o, vbb_o,
                 qi_o, kif_o, kib_o, wi_o, ga_o, gb_o):
    x = x_ref[...]
    ms = jnp.mean(x * x, axis=-1, keepdims=True)
    xn = ((x * lax.rsqrt(ms + EPS)) * g_ref[...]).astype(BF16)
    ca, sa, cb, sb = cos_a[...], sin_a[...], cos_b[...], sin_b[...]

    def cat(chunks):
        return chunks[0] if len(chunks) == 1 else jnp.concatenate(chunks, axis=1)

    h = _dot(xn, wqa[...])
    qa_o[...] = cat(_rope_chunks(_group_rms(h, ones64, 1.0 / HD_A), gqa[...], ca, sa, HD_A // 2)).astype(BF16)
    h = _dot(xn, wka[...])
    ka = cat(_rope_chunks(_group_rms(h, ones64, 1.0 / HD_A), gka[...], ca, sa, HD_A // 2))
    kaf_o[...] = ka
    kab_o[...] = ka.astype(BF16)
    h = _dot(xn, wva[...])
    vaf_o[...] = h
    vab_o[...] = h.astype(BF16)
    h = _dot(xn, wqb[...])
    qb_o[...] = cat(_rope_chunks(_group_rms(h, ones128, 1.0 / HD_B), gqb[...], cb, sb, HD_B // 2)).astype(BF16)
    h = _dot(xn, wkb[...])
    kb = cat(_rope_chunks(_group_rms(h, ones128, 1.0 / HD_B), gkb[...], cb, sb, HD_B // 2))
    kbf_o[...] = kb
    kbb_o[...] = kb.astype(BF16)
    h = _dot(xn, wvb[...])
    vbf_o[...] = h
    vbb_o[...] = h.astype(BF16)
    h = _dot(xn, wqi[...])
    chunks = [h[:, c * LANES:(c + 1) * LANES] for c in range(W_QI // LANES)]
    ones_gain = jnp.full((1, W_QI), D_IDX ** -0.5, F32)
    qi_o[...] = cat(_rope_chunks(chunks, ones_gain, ca, sa, D_IDX // 2)).astype(BF16)
    h = _dot(xn, wkw[...])
    ki = _rope_chunks(_group_rms(h, ones64, 1.0 / D_IDX), gki[...], ca, sa, D_IDX // 2)[0]
    kif_o[...] = ki[:, :D_IDX]
    kib_o[...] = ki[:, :D_IDX].astype(BF16)
    wi_o[...] = h * (H_IDX ** -0.5)
    ga_o[...] = jax.nn.sigmoid(_dot(xn, wga[...]))
    gb_o[...] = jax.nn.sigmoid(_dot(xn, wgb[...]))


def _rope_tables(pos, half, group):
    inv_freq = ROPE_THETA ** (-jnp.arange(half, dtype=F32) / half)
    ang = pos.astype(F32)[:, None] * inv_freq[None, :]
    cos, sin = jnp.cos(ang), jnp.sin(ang)
    reps = LANES // group
    cos_t = jnp.tile(jnp.concatenate([cos, cos], axis=1), (1, reps))
    sin_t = jnp.tile(jnp.concatenate([-sin, sin], axis=1), (1, reps))
    return cos_t, sin_t


def _project(x, pos, pw, tm):
    t, d = x.shape
    cos_a, sin_a = _rope_tables(pos, HD_A // 2, HD_A)
    cos_b, sin_b = _rope_tables(pos, HD_B // 2, HD_B)
    row = lambda w: pl.BlockSpec((tm, w), lambda i: (i, 0))
    weights = [pw[k] for k in ("wqa", "wka", "wva", "wqb", "wkb", "wvb", "wqi", "wkw", "wga", "wgb")]
    consts = [pw[k] for k in ("gqa", "gka", "gqb", "gkb", "gki")]
    in_specs = ([row(d), _const_spec((1, d))] + [_const_spec(w.shape) for w in weights]
                + [_const_spec(c.shape) for c in consts] + [row(LANES)] * 4
                + [_const_spec((LANES, LANES))] * 2)
    outs = [(W_QA, BF16), (W_KA, F32), (W_KA, BF16), (W_VA, F32), (W_VA, BF16),
            (W_QB, BF16), (W_KB, F32), (W_KB, BF16), (W_KB, F32), (W_KB, BF16),
            (W_QI, BF16), (D_IDX, F32), (D_IDX, BF16), (LANES, F32), (d, F32), (d, F32)]
    return pl.pallas_call(
        _proj_kernel,
        out_shape=[jax.ShapeDtypeStruct((t, w), dt) for w, dt in outs],
        grid=(t // tm,),
        in_specs=in_specs,
        out_specs=[row(w) for w, _ in outs],
        compiler_params=_cparams(("parallel",)),
        name="proj",
    )(x, pw["g_attn"], *weights, *consts, cos_a, sin_a, cos_b, sin_b, pw["ones64"], pw["ones128"])


def _lane_rep(col):
    return jnp.broadcast_to(col, (col.shape[0], LANES))


def _online_softmax_step(s, v, m_ref, l_ref, acc_ref, idx):
    m_old = m_ref[idx]
    m_new = jnp.maximum(m_old, _lane_rep(jnp.max(s, axis=1, keepdims=True)))
    alpha = jnp.exp2(m_old - m_new)
    p = jnp.exp2(s - m_new[:, :1])
    l_ref[idx] = alpha * l_ref[idx] + _lane_rep(jnp.sum(p, axis=1, keepdims=True))
    acc_ref[idx] = alpha * acc_ref[idx] + _dot(p.astype(BF16), v)
    m_ref[idx] = m_new


def _bounded_softmax_step(s, keep, v, l_ref, acc_ref, idx):
    p = jnp.exp2(s)
    if keep is not None:
        p = jnp.where(keep, p, 0.0)
    part = p[:, :LANES]
    for c in range(1, p.shape[1] // LANES):
        part = part + p[:, c * LANES:(c + 1) * LANES]
    l_ref[idx] = l_ref[idx] + part
    acc_ref[idx] = acc_ref[idx] + _dot(p.astype(BF16), v)


def _row_sum(l, online):
    return l[:, :1] if online else jnp.sum(l, axis=1, keepdims=True)


def _for_each_tile(nt, fn):
    def pair(jj, carry):
        fn(2 * jj)
        fn(2 * jj + 1)
        return carry

    lax.fori_loop(0, nt // 2, pair, 0)

    @pl.when(nt % 2 == 1)
    def _():
        fn(nt - 1)


def _split_components(qg):
    lane = lax.broadcasted_iota(I32, qg.shape, 1)
    zero = jnp.zeros_like(qg)
    return jnp.concatenate([jnp.where(lane < HD_A, qg, zero), jnp.where(lane >= HD_A, qg, zero)], axis=0)


def _lambda(lq1, lk1, lq2, lk2, lam_init):
    a = jnp.exp(jnp.sum(lq1[...] * lk1[...], axis=1, keepdims=True))
    b = jnp.exp(jnp.sum(lq2[...] * lk2[...], axis=1, keepdims=True))
    return a - b + lam_init


def _pdiff_kernel(lq1, lk1, lq2, lk2, q_ref, k_ref, v_ref, o_ref, lhs, m_s, l_s, acc,
                  *, tq, tk, lam_init, online):
    i = pl.program_id(0)
    for hg in range(H_A):
        lhs[hg] = _split_components(q_ref[:, hg * LANES:(hg + 1) * LANES])
    m_s[...] = jnp.full(m_s.shape, NEG_BIG, F32)
    l_s[...] = jnp.zeros(l_s.shape, F32)
    acc[...] = jnp.zeros(acc.shape, F32)

    def tile(j, masked):
        start = pl.multiple_of(j * tk, tk)
        for kv in range(KV_A):
            kt = k_ref[pl.ds(start, tk), kv * LANES:(kv + 1) * LANES]
            vt = v_ref[pl.ds(start, tk), kv * LANES:(kv + 1) * LANES]
            for g in range(G_A):
                hg = kv * G_A + g
                s = _dot_t(lhs[hg], kt)
                keep = None
                if masked:
                    r = i * tq + lax.broadcasted_iota(I32, s.shape, 0) % tq
                    c = j * tk + lax.broadcasted_iota(I32, s.shape, 1)
                    keep = c <= r
                if online:
                    if masked:
                        s = jnp.where(keep, s, -jnp.inf)
                    _online_softmax_step(s, vt, m_s, l_s, acc, hg)
                else:
                    _bounded_softmax_step(s, keep, vt, l_s, acc, hg)

    def body(j, carry):
        tile(j, False)
        return carry

    n_full = (i * tq) // tk
    lax.fori_loop(0, n_full, body, 0)
    tile(n_full, True)
    lam = _lambda(lq1, lk1, lq2, lk2, lam_init)
    for hg in range(H_A):
        o = acc[hg] / _row_sum(l_s[hg], online)
        o_ref[:, hg * LANES:(hg + 1) * LANES] = o[:tq] - lam * o[tq:]


def _prompt_diff_attn(lams, qa, ka, va, lam_init, tq, tk, online):
    t = qa.shape[0]
    assert tk % tq == 0
    kern = functools.partial(_pdiff_kernel, tq=tq, tk=tk, lam_init=lam_init, online=online)
    return pl.pallas_call(
        kern,
        out_shape=jax.ShapeDtypeStruct((t, H_A * DV_A), F32),
        grid=(t // tq,),
        in_specs=[_const_spec((1, HD_A))] * 4 + [pl.BlockSpec((tq, W_QA), lambda i: (i, 0)),
                                                  _const_spec(ka.shape), _const_spec(va.shape)],
        out_specs=pl.BlockSpec((tq, H_A * DV_A), lambda i: (i, 0)),
        scratch_shapes=[pltpu.VMEM((H_A, 2 * tq, LANES), BF16), pltpu.VMEM((H_A, 2 * tq, LANES), F32),
                        pltpu.VMEM((H_A, 2 * tq, LANES), F32), pltpu.VMEM((H_A, 2 * tq, LANES), F32)],
        compiler_params=_cparams(("parallel",)),
        name="prompt_diff_attn",
    )(*lams, qa, ka, va)


def _sortable_key(score):
    bits = pltpu.bitcast(score, I32)
    return bits ^ ((bits >> 31) & 0x7FFFFFFF)


I16_MIN, I16_MAX = -(2 ** 15), 2 ** 15 - 1


def _split_halves_t(key):
    kt = key.T
    return (kt >> 16).astype(I16), ((kt & 0xFFFF) + I16_MIN).astype(I16)


def _kth_largest_half(halves, nt, rows, tk, topk):
    fold = tk // 2 if tk % 64 == 0 else tk

    def count_ge(cand):
        c16 = cand.astype(I16)

        def body(j, cnt):
            hit = jnp.where(halves[j] >= c16, jnp.int16(1), jnp.int16(0))
            for c in range(tk // fold):
                cnt = cnt + hit[c * fold:(c + 1) * fold]
            return cnt

        cnt = lax.fori_loop(0, nt, body, jnp.zeros((fold, rows), I16))
        part = cnt[:16]
        for c in range(1, fold // 16):
            part = part + cnt[c * 16:(c + 1) * 16]
        return jnp.sum(part.astype(I32), axis=0, keepdims=True)

    zero = jnp.zeros((1, rows), I32)
    t = jnp.where(count_ge(zero) >= topk, zero, jnp.full((1, rows), I16_MIN, I32))

    def body(b, t):
        cand = t | jnp.left_shift(jnp.int32(1), 14 - b)
        return jnp.where(count_ge(cand) >= topk, cand, t)

    return lax.fori_loop(0, 15, body, t)


def _row_to_col(v):
    n = v.shape[1]
    eye = lax.broadcasted_iota(I32, (n, n), 0) == lax.broadcasted_iota(I32, (n, n), 1)
    return jnp.sum(jnp.where(eye, jnp.broadcast_to(v, (n, n)), 0), axis=1, keepdims=True)


def _topk_threshold(hi_t, lo_t, nt, rows, tk, topk):
    top = _kth_largest_half(hi_t, nt, rows, tk, topk)
    top16 = top.astype(I16)

    def low_tile(j, carry):
        hi = hi_t[j]
        lo_t[j] = jnp.where(hi == top16, lo_t[j], jnp.where(hi > top16, jnp.int16(I16_MAX), jnp.int16(I16_MIN)))
        return carry

    lax.fori_loop(0, nt, low_tile, 0)
    low = _kth_largest_half(lo_t, nt, rows, tk, topk)
    return jnp.maximum(_row_to_col(top) * 65536 + (_row_to_col(low) - I16_MIN), INT_MIN + 1)


def _pdsa_kernel(qi_ref, wi_ref, qb_ref, ki_ref, kb_ref, vb_ref, o_ref, keys, hi_t, lo_t, m_s, l_s, acc,
                 *, tq, tk, topk, online):
    i = pl.program_id(0)
    nt = (i * tq + tq + tk - 1) // tk
    row = i * tq + lax.broadcasted_iota(I32, (tq, tk), 0)
    col0 = lax.broadcasted_iota(I32, (tq, tk), 1)

    qi_st = jnp.concatenate([qi_ref[:, h * D_IDX:(h + 1) * D_IDX] for h in range(H_IDX)], axis=0)
    wv = wi_ref[...]
    w_cols = [wv[:, D_IDX + h:D_IDX + h + 1] for h in range(H_IDX)]

    def score_tile(j):
        start = pl.multiple_of(j * tk, tk)
        lg = _dot_t(qi_st, ki_ref[pl.ds(start, tk), :])
        sc = jnp.zeros((tq, tk), F32)
        for h in range(H_IDX):
            sc = sc + jnp.maximum(lg[h * tq:(h + 1) * tq], 0.0) * w_cols[h]
        key = jnp.where(col0 + j * tk <= row, _sortable_key(sc), INT_MIN)
        keys[j] = key
        hi_t[j], lo_t[j] = _split_halves_t(key)

    _for_each_tile(nt, score_tile)

    thr = _topk_threshold(hi_t, lo_t, nt, tq, tk, topk)

    m_s[...] = jnp.full(m_s.shape, NEG_BIG, F32)
    l_s[...] = jnp.zeros(l_s.shape, F32)
    acc[...] = jnp.zeros(acc.shape, F32)
    q_st = [jnp.concatenate([qb_ref[:, (kv * G_B + g) * HD_B:(kv * G_B + g + 1) * HD_B] for g in range(G_B)], axis=0)
            for kv in range(KV_B)]

    def attn_tile(j):
        start = pl.multiple_of(j * tk, tk)
        kt = keys[j]
        sel = kt >= thr
        sel = jnp.concatenate([sel] * G_B, axis=0)
        for kv in range(KV_B):
            s = _dot_t(q_st[kv], kb_ref[pl.ds(start, tk), kv * HD_B:(kv + 1) * HD_B])
            vt = vb_ref[pl.ds(start, tk), kv * HD_B:(kv + 1) * HD_B]
            if online:
                _online_softmax_step(jnp.where(sel, s, -jnp.inf), vt, m_s, l_s, acc, kv)
            else:
                _bounded_softmax_step(s, sel, vt, l_s, acc, kv)

    _for_each_tile(nt, attn_tile)
    for kv in range(KV_B):
        o = acc[kv] / _row_sum(l_s[kv], online)
        for g in range(G_B):
            hg = kv * G_B + g
            o_ref[:, hg * HD_B:(hg + 1) * HD_B] = o[g * tq:(g + 1) * tq]


def _prompt_dsa(qi, wi, qb, ki, kb, vb, tq, tk, online):
    t = qb.shape[0]
    topk = min(TOPK_MAX, t // 4)
    kern = functools.partial(_pdsa_kernel, tq=tq, tk=tk, topk=topk, online=online)
    row = lambda w: pl.BlockSpec((tq, w), lambda i: (i, 0))
    return pl.pallas_call(
        kern,
        out_shape=jax.ShapeDtypeStruct((t, H_B * HD_B), F32),
        grid=(t // tq,),
        in_specs=[row(W_QI), row(LANES), row(W_QB), _const_spec(ki.shape), _const_spec(kb.shape),
                  _const_spec(vb.shape)],
        out_specs=row(H_B * HD_B),
        scratch_shapes=[pltpu.VMEM((t // tk, tq, tk), I32), pltpu.VMEM((t // tk, tk, tq), I16),
                        pltpu.VMEM((t // tk, tk, tq), I16), pltpu.VMEM((KV_B, G_B * tq, LANES), F32),
                        pltpu.VMEM((KV_B, G_B * tq, LANES), F32), pltpu.VMEM((KV_B, G_B * tq, LANES), F32)],
        compiler_params=_cparams(("parallel",)),
        name="prompt_dsa",
    )(qi, wi, qb, ki, kb, vb)


def _fetch_pages(pt_ref, pools, bufs, sems, *, pps, nsteps, total):
    t = pl.program_id(0) * nsteps + pl.program_id(1)
    slot = t % 2

    def copies(tt, s, p):
        pg = pt_ref[tt // nsteps, (tt % nsteps) * pps + p]
        return [pltpu.make_async_copy(pool.at[pg], buf.at[s, p], sem.at[s])
                for pool, buf, sem in zip(pools, bufs, sems)]

    def start(tt, s):
        def body(p, carry):
            for cp in copies(tt, s, p):
                cp.start()
            return carry
        lax.fori_loop(0, pps, body, 0)

    @pl.when(t == 0)
    def _():
        start(t, slot)

    @pl.when(t + 1 < total)
    def _():
        start(t + 1, 1 - slot)

    def wait(p, carry):
        for cp in copies(t, slot, p):
            cp.wait()
        return carry

    lax.fori_loop(0, pps, wait, 0)
    return slot


def _page_scratch(pools, pps):
    bufs = [pltpu.VMEM((2, pps) + pool.shape[1:], pool.dtype) for pool in pools]
    sems = [pltpu.SemaphoreType.DMA((2,)) for _ in pools]
    return bufs + sems


def _sdiff_kernel(pt_ref, lq1, lk1, lq2, lk2, q_ref, kn_ref, vn_ref, kpool, vpool, o_ref,
                  kbuf, vbuf, ksem, vsem, kc, vc, lhs, m_s, l_s, acc, *, pps, page, tn, lam_init, nsteps, total):
    j = pl.program_id(1)

    @pl.when(j == 0)
    def _():
        q = q_ref[0].astype(F32)
        for kv in range(KV_A):
            parts = [_split_components(q[:, (kv * G_A + g) * LANES:(kv * G_A + g + 1) * LANES]) for g in range(G_A)]
            lhs[kv] = jnp.concatenate(parts, axis=0).astype(BF16)
        m_s[...] = jnp.full(m_s.shape, NEG_BIG, F32)
        l_s[...] = jnp.zeros(l_s.shape, F32)
        acc[...] = jnp.zeros(acc.shape, F32)

    slot = _fetch_pages(pt_ref, [kpool, vpool], [kbuf, vbuf], [ksem, vsem], pps=pps, nsteps=nsteps, total=total)
    for p in range(pps):
        for kv in range(KV_A):
            kc[kv, :, p * page:(p + 1) * page] = kbuf[slot, p, kv].reshape(2 * HD_A, page).astype(BF16)
            vc[kv, p * page:(p + 1) * page, :] = vbuf[slot, p, pl.ds(kv, page, stride=KV_A), :].astype(BF16)
    for kv in range(KV_A):
        s = _dot(lhs[kv], kc[kv])
        _online_softmax_step(s, vc[kv], m_s, l_s, acc, kv)

    @pl.when(j == pl.num_programs(1) - 1)
    def _():
        kn = kn_ref[0]
        vn = vn_ref[0]
        lam = _lambda(lq1, lk1, lq2, lk2, lam_init)
        for kv in range(KV_A):
            s = _dot_t(lhs[kv], kn[:, kv * LANES:(kv + 1) * LANES])
            r = lax.broadcasted_iota(I32, s.shape, 0) % tn
            c = lax.broadcasted_iota(I32, s.shape, 1)
            s = jnp.where(c <= r, s, -jnp.inf)
            _online_softmax_step(s, vn[:, kv * LANES:(kv + 1) * LANES], m_s, l_s, acc, kv)
            o = acc[kv] / l_s[kv]
            for g in range(G_A):
                hg = kv * G_A + g
                o_ref[0, :, hg * LANES:(hg + 1) * LANES] = (o[(2 * g) * tn:(2 * g + 1) * tn]
                                                            - lam * o[(2 * g + 1) * tn:(2 * g + 2) * tn])


def _sample_diff_attn(page_table, lams, qa, ka_new, va_new, pool_kt, pool_v, lam_init, pps):
    nb, tn, _ = qa.shape
    n_pages = page_table.shape[1]
    page = pool_kt.shape[-1]
    nsteps = n_pages // pps
    kern = functools.partial(_sdiff_kernel, pps=pps, page=page, tn=tn, lam_init=lam_init, nsteps=nsteps,
                             total=nb * nsteps)
    seq = lambda w: pl.BlockSpec((1, tn, w), lambda b, j, pt: (b, 0, 0))
    cst = lambda shape: pl.BlockSpec(shape, lambda b, j, pt: (0,) * len(shape))
    hbm = pl.BlockSpec(memory_space=pl.ANY)
    rows = 2 * G_A * tn
    return pl.pallas_call(
        kern,
        out_shape=jax.ShapeDtypeStruct((nb, tn, H_A * DV_A), F32),
        grid_spec=pltpu.PrefetchScalarGridSpec(
            num_scalar_prefetch=1,
            grid=(nb, nsteps),
            in_specs=[cst((1, HD_A))] * 4 + [seq(W_QA), seq(W_KA), seq(W_VA), hbm, hbm],
            out_specs=seq(H_A * DV_A),
            scratch_shapes=_page_scratch([pool_kt, pool_v], pps)
                           + [pltpu.VMEM((KV_A, 2 * HD_A, pps * page), BF16),
                              pltpu.VMEM((KV_A, pps * page, DV_A), BF16),
                              pltpu.VMEM((KV_A, rows, LANES), BF16), pltpu.VMEM((KV_A, rows, LANES), F32),
                              pltpu.VMEM((KV_A, rows, LANES), F32), pltpu.VMEM((KV_A, rows, LANES), F32)]),
        compiler_params=_cparams(("arbitrary", "arbitrary")),
        name="sample_diff_attn",
    )(page_table, *lams, qa, ka_new, va_new, pool_kt, pool_v)


def _sscore_kernel(pt_ref, qi_ref, wi_ref, kin_ref, ipool, keys_o, newkeys_o, ibuf, isem,
                   *, pps, page, tn, tile, nsteps, total):
    j = pl.program_id(1)
    q = qi_ref[0].astype(F32)
    qi_st = jnp.concatenate([q[:, h * D_IDX:(h + 1) * D_IDX] for h in range(H_IDX)], axis=0).astype(BF16)
    wv = wi_ref[0]
    w_cols = [wv[:, D_IDX + h:D_IDX + h + 1] for h in range(H_IDX)]

    def scores(lg):
        sc = jnp.zeros((tn, lg.shape[1]), F32)
        for h in range(H_IDX):
            sc = sc + jnp.maximum(lg[h * tn:(h + 1) * tn], 0.0) * w_cols[h]
        return _sortable_key(sc)

    slot = _fetch_pages(pt_ref, [ipool], [ibuf], [isem], pps=pps, nsteps=nsteps, total=total)
    kt = jnp.concatenate([ibuf[slot, p] for p in range(pps)], axis=1).astype(BF16)
    keys = scores(_dot(qi_st, kt))
    for c in range(pps * page // tile):
        keys_o[c] = keys[:, c * tile:(c + 1) * tile]

    @pl.when(j == nsteps - 1)
    def _():
        nk = scores(_dot_t(qi_st, kin_ref[0]))
        r = lax.broadcasted_iota(I32, nk.shape, 0)
        c = lax.broadcasted_iota(I32, nk.shape, 1)
        nk = jnp.where(c <= r, nk, INT_MIN)
        newkeys_o[...] = jnp.concatenate([nk, jnp.full((tn, LANES - tn), INT_MIN, I32)], axis=1)


def _sample_scores(page_table, qi, wi, ki_new, pool_it, pps, tile):
    nb, tn, _ = qi.shape
    n_pages = page_table.shape[1]
    page = pool_it.shape[2]
    nsteps = n_pages // pps
    tiles_per_step = pps * page // tile
    kern = functools.partial(_sscore_kernel, pps=pps, page=page, tn=tn, tile=tile, nsteps=nsteps,
                             total=nb * nsteps)
    seq = lambda w: pl.BlockSpec((1, tn, w), lambda b, j, pt: (b, 0, 0))
    return pl.pallas_call(
        kern,
        out_shape=[jax.ShapeDtypeStruct((n_pages * page // tile, nb * tn, tile), I32),
                   jax.ShapeDtypeStruct((nb * tn, LANES), I32)],
        grid_spec=pltpu.PrefetchScalarGridSpec(
            num_scalar_prefetch=1,
            grid=(nb, nsteps),
            in_specs=[seq(W_QI), seq(LANES), seq(D_IDX), pl.BlockSpec(memory_space=pl.ANY)],
            out_specs=[pl.BlockSpec((tiles_per_step, tn, tile), lambda b, j, pt: (j, b, 0)),
                       pl.BlockSpec((tn, LANES), lambda b, j, pt: (b, 0))],
            scratch_shapes=_page_scratch([pool_it], pps)),
        compiler_params=_cparams(("arbitrary", "arbitrary")),
        name="sample_scores",
    )(page_table, qi, wi, ki_new, pool_it)


def _sthr_kernel(keys_ref, newkeys_ref, sel_ref, hi_t, lo_t, *, rows, tile, nt, topk):
    def split(j, carry):
        hi_t[j], lo_t[j] = _split_halves_t(keys_ref[j])
        return carry

    lax.fori_loop(0, nt - 1, split, 0)
    pad = [jnp.full((rows, tile - LANES), INT_MIN, I32)] if tile > LANES else []
    hi_t[nt - 1], lo_t[nt - 1] = _split_halves_t(jnp.concatenate([newkeys_ref[...]] + pad, axis=1))
    thr = _topk_threshold(hi_t, lo_t, nt, rows, tile, topk)
    for j in range(nt - 1):
        sel_ref[:, j * tile:(j + 1) * tile] = (keys_ref[j] >= thr).astype(F32)
    sel_ref[:, (nt - 1) * tile:] = (newkeys_ref[...] >= thr).astype(F32)


def _sample_select(keys, newkeys, topk):
    n_tiles, rows_all, tile = keys.shape
    rows = _tile(rows_all, 128)
    nt = n_tiles + 1
    kern = functools.partial(_sthr_kernel, rows=rows, tile=tile, nt=nt, topk=topk)
    width = n_tiles * tile + LANES
    return pl.pallas_call(
        kern,
        out_shape=jax.ShapeDtypeStruct((rows_all, width), F32),
        grid=(rows_all // rows,),
        in_specs=[pl.BlockSpec((n_tiles, rows, tile), lambda i: (0, i, 0)),
                  pl.BlockSpec((rows, LANES), lambda i: (i, 0))],
        out_specs=pl.BlockSpec((rows, width), lambda i: (i, 0)),
        scratch_shapes=[pltpu.VMEM((nt, tile, rows), I16), pltpu.VMEM((nt, tile, rows), I16)],
        compiler_params=_cparams(("parallel",)),
        name="sample_select",
    )(keys, newkeys)


def _sdsa_kernel(pt_ref, q_ref, sel_ref, seln_ref, kn_ref, vn_ref, kpool, vpool, o_ref,
                 kbuf, vbuf, ksem, vsem, kc, vc, lhs, m_s, l_s, acc, *, pps, page, tn, nsteps, total):
    j = pl.program_id(1)

    @pl.when(j == 0)
    def _():
        q = q_ref[0].astype(F32)
        for kv in range(KV_B):
            parts = [q[:, (kv * G_B + g) * HD_B:(kv * G_B + g + 1) * HD_B] for g in range(G_B)]
            lhs[kv] = jnp.concatenate(parts, axis=0).astype(BF16)
        m_s[...] = jnp.full(m_s.shape, NEG_BIG, F32)
        l_s[...] = jnp.zeros(l_s.shape, F32)
        acc[...] = jnp.zeros(acc.shape, F32)

    def step(sel, kmats, vmats):
        sel = jnp.concatenate([sel > 0.5] * G_B, axis=0)
        for kv in range(KV_B):
            s = jnp.where(sel, _dot_t(lhs[kv], kmats[kv]), -jnp.inf)
            _online_softmax_step(s, vmats[kv], m_s, l_s, acc, kv)

    slot = _fetch_pages(pt_ref, [kpool, vpool], [kbuf, vbuf], [ksem, vsem], pps=pps, nsteps=nsteps, total=total)
    for p in range(pps):
        for kv in range(KV_B):
            kc[kv, p * page:(p + 1) * page, :] = kbuf[slot, p, pl.ds(kv, page, stride=KV_B), :].astype(BF16)
            vc[kv, p * page:(p + 1) * page, :] = vbuf[slot, p, pl.ds(kv, page, stride=KV_B), :].astype(BF16)
    step(sel_ref[0], [kc[kv] for kv in range(KV_B)], [vc[kv] for kv in range(KV_B)])

    @pl.when(j == nsteps - 1)
    def _():
        kn, vn = kn_ref[0], vn_ref[0]
        step(seln_ref[0][:, :tn], [kn[:, kv * HD_B:(kv + 1) * HD_B] for kv in range(KV_B)],
             [vn[:, kv * HD_B:(kv + 1) * HD_B] for kv in range(KV_B)])
        for kv in range(KV_B):
            o = acc[kv] / l_s[kv]
            for g in range(G_B):
                hg = kv * G_B + g
                o_ref[0, :, hg * HD_B:(hg + 1) * HD_B] = o[g * tn:(g + 1) * tn]


def _sample_dsa(page_table, qb, sel, kb_new, vb_new, pool_k, pool_v, pps):
    nb, tn, _ = qb.shape
    n_pages = page_table.shape[1]
    page = pool_k.shape[1] // KV_B
    tk = pps * page
    nsteps = n_pages // pps
    kern = functools.partial(_sdsa_kernel, pps=pps, page=page, tn=tn, nsteps=nsteps, total=nb * nsteps)
    seq = lambda w: pl.BlockSpec((1, tn, w), lambda b, j, pt: (b, 0, 0))
    hbm = pl.BlockSpec(memory_space=pl.ANY)
    rows = G_B * tn
    return pl.pallas_call(
        kern,
        out_shape=jax.ShapeDtypeStruct((nb, tn, H_B * HD_B), F32),
        grid_spec=pltpu.PrefetchScalarGridSpec(
            num_scalar_prefetch=1,
            grid=(nb, nsteps),
            in_specs=[seq(W_QB), pl.BlockSpec((1, tn, tk), lambda b, j, pt: (b, 0, j)),
                      pl.BlockSpec((1, tn, LANES), lambda b, j, pt: (b, 0, (n_pages * page) // LANES)),
                      seq(W_KB), seq(W_KB), hbm, hbm],
            out_specs=seq(H_B * HD_B),
            scratch_shapes=_page_scratch([pool_k, pool_v], pps)
                           + [pltpu.VMEM((KV_B, tk, HD_B), BF16), pltpu.VMEM((KV_B, tk, HD_B), BF16),
                              pltpu.VMEM((KV_B, rows, LANES), BF16), pltpu.VMEM((KV_B, rows, LANES), F32),
                              pltpu.VMEM((KV_B, rows, LANES), F32), pltpu.VMEM((KV_B, rows, LANES), F32)]),
        compiler_params=_cparams(("arbitrary", "arbitrary")),
        name="sample_dsa",
    )(page_table, qb, sel, sel, kb_new, vb_new, pool_k, pool_v)


def _finish_kernel(x_ref, oa_ref, ob_ref, ga_ref, gb_ref, sub_ref, gm_ref, wa, wb, wo, wu, wd, y_ref, *, lam_init):
    oa = oa_ref[...]
    sub = sub_ref[...]
    chunks = []
    for h in range(H_A):
        c = oa[:, h * DV_A:(h + 1) * DV_A]
        ms = jnp.mean(c * c, axis=-1, keepdims=True)
        chunks.append(((c * lax.rsqrt(ms + EPS)) * sub) * (1.0 - lam_init))
    oan = jnp.concatenate(chunks, axis=1).astype(BF16)
    pa = _dot(oan, wa[...])
    pb = _dot(ob_ref[...].astype(BF16), wb[...])
    mix = (ga_ref[...] * pa + gb_ref[...] * pb).astype(BF16)
    x1 = x_ref[...] + _dot(mix, wo[...])
    ms = jnp.mean(x1 * x1, axis=-1, keepdims=True)
    xn = ((x1 * lax.rsqrt(ms + EPS)) * gm_ref[...]).astype(BF16)
    hid = jnp.maximum(_dot(xn, wu[...]), 0.0)
    y_ref[...] = x1 + _dot((hid * hid).astype(BF16), wd[...])


def _finish(x, oa, ob, ga, gb, pw, lam_init, tm):
    t, d = x.shape
    row = pl.BlockSpec((tm, d), lambda i: (i, 0))
    weights = [pw[k] for k in ("w_a_proj", "w_b_proj", "w_out", "w_up", "w_down")]
    kern = functools.partial(_finish_kernel, lam_init=lam_init)
    return pl.pallas_call(
        kern,
        out_shape=jax.ShapeDtypeStruct((t, d), F32),
        grid=(t // tm,),
        in_specs=[row] * 5 + [_const_spec((1, DV_A)), _const_spec((1, d))] + [_const_spec(w.shape) for w in weights],
        out_specs=row,
        compiler_params=_cparams(("parallel",)),
        name="finish",
    )(x, oa, ob, ga, gb, pw["sub_norm_a"], pw["g_mlp"], *weights)


def _prep_weights(l, g_attn_norm, w_in, q_norm_a, k_norm_a, sub_norm_a, q_norm_b, k_norm_b, idx_k_norm,
                  w_a_proj, w_b_proj, w_out, g_mlp_norm, w_up, w_down):
    w = w_in[l].astype(BF16)
    widths = (W_QA, W_KA, W_VA, W_QB, W_KB, W_KB, W_QI, D_IDX, H_IDX, w.shape[0], w.shape[0])
    offs = np.concatenate([[0], np.cumsum(widths)])
    seg = [w[:, int(offs[i]):int(offs[i + 1])] for i in range(len(widths))]
    wkw = jnp.concatenate([seg[7], seg[8], jnp.zeros((w.shape[0], LANES - D_IDX - H_IDX), BF16)], axis=1)
    lane = np.arange(LANES)
    row2 = lambda v: v.astype(F32)[None, :]
    return {
        "g_attn": row2(g_attn_norm[l]),
        "wqa": seg[0], "wka": seg[1], "wva": seg[2], "wqb": seg[3], "wkb": seg[4], "wvb": seg[5],
        "wqi": seg[6], "wkw": wkw, "wga": seg[9], "wgb": seg[10],
        "gqa": row2(jnp.tile(q_norm_a[l], W_QA // HD_A) * (HD_A ** -0.5 * LOG2E)),
        "gka": row2(jnp.tile(k_norm_a[l], W_KA // HD_A)),
        "gqb": row2(jnp.tile(q_norm_b[l], W_QB // HD_B) * (HD_B ** -0.5 * LOG2E)),
        "gkb": row2(jnp.tile(k_norm_b[l], W_KB // HD_B)),
        "gki": row2(jnp.concatenate([idx_k_norm[l], jnp.zeros((LANES - D_IDX,), F32)])),
        "ones64": jnp.asarray((lane[:, None] // HD_A) == (lane[None, :] // HD_A), BF16),
        "ones128": jnp.ones((LANES, LANES), BF16),
        "sub_norm_a": row2(sub_norm_a[l]), "g_mlp": row2(g_mlp_norm[l]),
        "w_a_proj": w_a_proj[l].astype(BF16), "w_b_proj": w_b_proj[l].astype(BF16),
        "w_out": w_out[l].astype(BF16), "w_up": w_up[l].astype(BF16), "w_down": w_down[l].astype(BF16),
    }


def _tile(n, pref):
    return pref if n % pref == 0 else n


def _tiles(seq, n_sample_rows, n_pages):
    pps = next(p for p in (32, 16, 8, 4, 2, 1) if n_pages % p == 0)
    return {"proj": _tile(seq, 256), "diff": _tile(seq, 256), "diff_k": _tile(seq, 512),
            "dsa_q": _tile(seq, 128), "dsa_k": _tile(seq, 512),
            "finish": _tile(seq, 256), "sample_rows": _tile(n_sample_rows, 256), "pps": pps,
            "sel_tile": _tile(pps * 128, 512)}


def _score_bound(q_gain, k_gain, head_dim):
    return (head_dim ** 0.5) * jnp.max(jnp.abs(q_gain)) * jnp.max(jnp.abs(k_gain))


def kernel(x_prompt, x_sample, cache_a_k, cache_a_v, cache_b_k, cache_b_v, cache_b_idx_k, page_table, g_attn_norm, w_in, q_norm_a, k_norm_a, lam_q1, lam_k1, lam_q2, lam_k2, sub_norm_a, q_norm_b, k_norm_b, idx_k_norm, w_a_proj, w_b_proj, w_out, g_mlp_norm, w_up, w_down):
    depth = w_in.shape[0]
    nb_p, seq, d = x_prompt.shape
    nb_s, tn, _ = x_sample.shape
    n_pages = page_table.shape[1]
    n_phys, page = cache_a_k.shape[1], cache_a_k.shape[2]
    past = n_pages * page
    assert nb_p == 1
    pos_p = jnp.arange(seq)
    pos_s = past + (jnp.arange(nb_s * tn) % tn)
    tl = _tiles(seq, nb_s * tn, n_pages)
    pps = tl["pps"]

    xp = x_prompt.reshape(seq, d)
    xs = x_sample.reshape(nb_s * tn, d)
    outs_p = [[] for _ in range(5)]
    outs_s = [[] for _ in range(5)]
    for l in range(depth):
        lam_init = 0.8 - 0.6 * math.exp(-0.3 * l)
        pw = _prep_weights(l, g_attn_norm, w_in, q_norm_a, k_norm_a, sub_norm_a, q_norm_b, k_norm_b, idx_k_norm,
                           w_a_proj, w_b_proj, w_out, g_mlp_norm, w_up, w_down)
        lams = [v[l].astype(F32)[None, :] for v in (lam_q1, lam_k1, lam_q2, lam_k2)]

        (qa, kaf, kab, vaf, vab, qb, kbf, kbb, vbf, vbb, qi, kif, kib, wi, ga, gb) = _project(
            xs, pos_s, pw, tl["sample_rows"])
        seq3 = lambda a: a.reshape(nb_s, tn, a.shape[-1])
        pool_akt = jnp.transpose(cache_a_k[l], (0, 2, 3, 4, 1))
        pool_it = jnp.transpose(cache_b_idx_k[l], (0, 2, 1))
        rows2 = lambda c: c[l].reshape(n_phys, page * c.shape[3], c.shape[4])
        oa = _sample_diff_attn(page_table, lams, seq3(qa), seq3(kab), seq3(vab), pool_akt, rows2(cache_a_v),
                               lam_init, pps)
        keys, newkeys = _sample_scores(page_table, seq3(qi), seq3(wi), seq3(kib), pool_it, pps, tl["sel_tile"])
        sel = seq3(_sample_select(keys, newkeys, min(TOPK_MAX, (past + tn) // 4)))
        ob = _sample_dsa(page_table, seq3(qb), sel, seq3(kbb), seq3(vbb), rows2(cache_b_k), rows2(cache_b_v), pps)
        xs = _finish(xs, oa.reshape(nb_s * tn, -1), ob.reshape(nb_s * tn, -1), ga, gb, pw, lam_init,
                     tl["sample_rows"])
        for lst, v in zip(outs_s, (kaf.reshape(nb_s, tn, KV_A, 2, HD_A), vaf.reshape(nb_s, tn, KV_A, DV_A),
                                   kbf.reshape(nb_s, tn, KV_B, HD_B), vbf.reshape(nb_s, tn, KV_B, HD_B),
                                   kif.reshape(nb_s, tn, D_IDX))):
            lst.append(v)

        (qa, kaf, kab, vaf, vab, qb, kbf, kbb, vbf, vbb, qi, kif, kib, wi, ga, gb) = _project(
            xp, pos_p, pw, tl["proj"])
        diff = lambda online: functools.partial(_prompt_diff_attn, lams, qa, kab, vab, lam_init, tl["diff"], tl["diff_k"],
                                                online)
        oa = lax.cond(_score_bound(q_norm_a[l], k_norm_a[l], HD_A) <= MAX_BOUNDED_SCORE, diff(False), diff(True))
        dsa = lambda online: functools.partial(_prompt_dsa, qi, wi, qb, kib, kbb, vbb, tl["dsa_q"], tl["dsa_k"],
                                               online)
        ob = lax.cond(_score_bound(q_norm_b[l], k_norm_b[l], HD_B) <= MAX_BOUNDED_SCORE, dsa(False), dsa(True))
        xp = _finish(xp, oa, ob, ga, gb, pw, lam_init, tl["finish"])
        for lst, v in zip(outs_p, (kaf.reshape(1, seq, KV_A, 2, HD_A), vaf.reshape(1, seq, KV_A, DV_A),
                                   kbf.reshape(1, seq, KV_B, HD_B), vbf.reshape(1, seq, KV_B, HD_B),
                                   kif.reshape(1, seq, D_IDX))):
            lst.append(v)

    return (xp.reshape(nb_p, seq, d), xs.reshape(nb_s, tn, d),
            *[jnp.stack(v) for v in outs_p], *[jnp.stack(v) for v in outs_s])
```

```python
import functools
import math

import jax
import jax.numpy as jnp
import numpy as np
from jax import lax
from jax.experimental import pallas as pl
from jax.experimental.pallas import tpu as pltpu

F32 = jnp.float32
BF16 = jnp.bfloat16
I32 = jnp.int32
I16 = jnp.int16

LANES = 128
VMEM_LIMIT = 56 * 1024 * 1024
EPS = 1e-6
ROPE_THETA = 10000.0
TOPK_MAX = 256
NEG_BIG = -1e30
INT_MIN = -(2 ** 31)
LOG2E = 1.4426950408889634
MAX_BOUNDED_SCORE = 60.0

H_A, KV_A, HD_A = 8, 2, 64
G_A = H_A // KV_A
DV_A = 2 * HD_A
H_B, KV_B, HD_B = 8, 2, 128
G_B = H_B // KV_B
H_IDX, D_IDX = 8, 64
W_QA = H_A * 2 * HD_A
W_KA = KV_A * 2 * HD_A
W_VA = KV_A * DV_A
W_QB = H_B * HD_B
W_KB = KV_B * HD_B
W_QI = H_IDX * D_IDX


def _cparams(sem):
    return pltpu.CompilerParams(dimension_semantics=sem, vmem_limit_bytes=VMEM_LIMIT)


def _const_spec(shape):
    nd = len(shape)
    return pl.BlockSpec(shape, lambda *_: (0,) * nd, pipeline_mode=pl.Buffered(1))


def _dot_t(a, b):
    return lax.dot_general(a, b, (((1,), (1,)), ((), ())), preferred_element_type=F32)


def _dot(a, b):
    return jnp.dot(a, b, preferred_element_type=F32)


def _group_rms(h, ones_ref, inv_group):
    outs = []
    for c in range(h.shape[1] // LANES):
        hs = h[:, c * LANES:(c + 1) * LANES]
        sq = hs * hs
        hi = sq.astype(BF16)
        lo = (sq - hi.astype(F32)).astype(BF16)
        ss = _dot(hi, ones_ref[...]) + _dot(lo, ones_ref[...])
        outs.append(hs * lax.rsqrt(ss * inv_group + EPS))
    return outs


def _rope_chunks(chunks, gain, cos, sin, half):
    lane = lax.broadcasted_iota(I32, chunks[0].shape, 1)
    first = (lane % (2 * half)) < half
    outs = []
    for c, y in enumerate(chunks):
        y = y * gain[:, c * LANES:(c + 1) * LANES]
        if half == LANES // 2:
            partner = pltpu.roll(y, half, 1)
        else:
            partner = jnp.where(first, pltpu.roll(y, LANES - half, 1), pltpu.roll(y, half, 1))
        outs.append(y * cos + partner * sin)
    return outs


def _proj_kernel(x_ref, g_ref, wqa, wka, wva, wqb, wkb, wvb, wqi, wkw, wga, wgb,
                 gqa, gka, gqb, gkb, gki, cos_a, sin_a, cos_b, sin_b, ones64, ones128,
                 qa_o, kaf_o, kab_o, vaf_o, vab_o, qb_o, kbf_o, kbb_o, vbf_o, vbb_o,
                 qi_o, kif_o, kib_o, wi_o, ga_o, gb_o):
    x = x_ref[...]
    ms = jnp.mean(x * x, axis=-1, keepdims=True)
    xn = ((x * lax.rsqrt(ms + EPS)) * g_ref[...]).astype(BF16)
    ca, sa, cb, sb = cos_a[...], sin_a[...], cos_b[...], sin_b[...]

    def cat(chunks):
        return chunks[0] if len(chunks) == 1 else jnp.concatenate(chunks, axis=1)

    h = _dot(xn, wqa[...])
    qa_o[...] = cat(_rope_chunks(_group_rms(h, ones64, 1.0 / HD_A), gqa[...], ca, sa, HD_A // 2)).astype(BF16)
    h = _dot(xn, wka[...])
    ka = cat(_rope_chunks(_group_rms(h, ones64, 1.0 / HD_A), gka[...], ca, sa, HD_A // 2))
    kaf_o[...] = ka
    kab_o[...] = ka.astype(BF16)
    h = _dot(xn, wva[...])
    vaf_o[...] = h
    vab_o[...] = h.astype(BF16)
    h = _dot(xn, wqb[...])
    qb_o[...] = cat(_rope_chunks(_group_rms(h, ones128, 1.0 / HD_B), gqb[...], cb, sb, HD_B // 2)).astype(BF16)
    h = _dot(xn, wkb[...])
    kb = cat(_rope_chunks(_group_rms(h, ones128, 1.0 / HD_B), gkb[...], cb, sb, HD_B // 2))
    kbf_o[...] = kb
    kbb_o[...] = kb.astype(BF16)
    h = _dot(xn, wvb[...])
    vbf_o[...] = h
    vbb_o[...] = h.astype(BF16)
    h = _dot(xn, wqi[...])
    chunks = [h[:, c * LANES:(c + 1) * LANES] for c in range(W_QI // LANES)]
    ones_gain = jnp.full((1, W_QI), D_IDX ** -0.5, F32)
    qi_o[...] = cat(_rope_chunks(chunks, ones_gain, ca, sa, D_IDX // 2)).astype(BF16)
    h = _dot(xn, wkw[...])
    ki = _rope_chunks(_group_rms(h, ones64, 1.0 / D_IDX), gki[...], ca, sa, D_IDX // 2)[0]
    kif_o[...] = ki[:, :D_IDX]
    kib_o[...] = ki[:, :D_IDX].astype(BF16)
    wi_o[...] = h * (H_IDX ** -0.5)
    ga_o[...] = jax.nn.sigmoid(_dot(xn, wga[...]))
    gb_o[...] = jax.nn.sigmoid(_dot(xn, wgb[...]))


def _rope_tables(pos, half, group):
    inv_freq = ROPE_THETA ** (-jnp.arange(half, dtype=F32) / half)
    ang = pos.astype(F32)[:, None] * inv_freq[None, :]
    cos, sin = jnp.cos(ang), jnp.sin(ang)
    reps = LANES // group
    cos_t = jnp.tile(jnp.concatenate([cos, cos], axis=1), (1, reps))
    sin_t = jnp.tile(jnp.concatenate([-sin, sin], axis=1), (1, reps))
    return cos_t, sin_t


def _project(x, pos, pw, tm):
    t, d = x.shape
    cos_a, sin_a = _rope_tables(pos, HD_A // 2, HD_A)
    cos_b, sin_b = _rope_tables(pos, HD_B // 2, HD_B)
    row = lambda w: pl.BlockSpec((tm, w), lambda i: (i, 0))
    weights = [pw[k] for k in ("wqa", "wka", "wva", "wqb", "wkb", "wvb", "wqi", "wkw", "wga", "wgb")]
    consts = [pw[k] for k in ("gqa", "gka", "gqb", "gkb", "gki")]
    in_specs = ([row(d), _const_spec((1, d))] + [_const_spec(w.shape) for w in weights]
                + [_const_spec(c.shape) for c in consts] + [row(LANES)] * 4
                + [_const_spec((LANES, LANES))] * 2)
    outs = [(W_QA, BF16), (W_KA, F32), (W_KA, BF16), (W_VA, F32), (W_VA, BF16),
            (W_QB, BF16), (W_KB, F32), (W_KB, BF16), (W_KB, F32), (W_KB, BF16),
            (W_QI, BF16), (D_IDX, F32), (D_IDX, BF16), (LANES, F32), (d, F32), (d, F32)]
    return pl.pallas_call(
        _proj_kernel,
        out_shape=[jax.ShapeDtypeStruct((t, w), dt) for w, dt in outs],
        grid=(t // tm,),
        in_specs=in_specs,
        out_specs=[row(w) for w, _ in outs],
        compiler_params=_cparams(("parallel",)),
        name="proj",
    )(x, pw["g_attn"], *weights, *consts, cos_a, sin_a, cos_b, sin_b, pw["ones64"], pw["ones128"])


def _lane_rep(col):
    return jnp.broadcast_to(col, (col.shape[0], LANES))


def _online_softmax_step(s, v, m_ref, l_ref, acc_ref, idx):
    m_old = m_ref[idx]
    m_new = jnp.maximum(m_old, _lane_rep(jnp.max(s, axis=1, keepdims=True)))
    alpha = jnp.exp2(m_old - m_new)
    p = jnp.exp2(s - m_new[:, :1])
    l_ref[idx] = alpha * l_ref[idx] + _lane_rep(jnp.sum(p, axis=1, keepdims=True))
    acc_ref[idx] = alpha * acc_ref[idx] + _dot(p.astype(BF16), v)
    m_ref[idx] = m_new


def _bounded_softmax_step(s, keep, v, l_ref, acc_ref, idx):
    p = jnp.exp2(s)
    if keep is not None:
        p = jnp.where(keep, p, 0.0)
    part = p[:, :LANES]
    for c in range(1, p.shape[1] // LANES):
        part = part + p[:, c * LANES:(c + 1) * LANES]
    l_ref[idx] = l_ref[idx] + part
    acc_ref[idx] = acc_ref[idx] + _dot(p.astype(BF16), v)


def _row_sum(l, online):
    return l[:, :1] if online else jnp.sum(l, axis=1, keepdims=True)


def _for_each_tile(nt, fn):
    def pair(jj, carry):
        fn(2 * jj)
        fn(2 * jj + 1)
        return carry

    lax.fori_loop(0, nt // 2, pair, 0)

    @pl.when(nt % 2 == 1)
    def _():
        fn(nt - 1)


def _split_components(qg):
    lane = lax.broadcasted_iota(I32, qg.shape, 1)
    zero = jnp.zeros_like(qg)
    return jnp.concatenate([jnp.where(lane < HD_A, qg, zero), jnp.where(lane >= HD_A, qg, zero)], axis=0)


def _lambda(lq1, lk1, lq2, lk2, lam_init):
    a = jnp.exp(jnp.sum(lq1[...] * lk1[...], axis=1, keepdims=True))
    b = jnp.exp(jnp.sum(lq2[...] * lk2[...], axis=1, keepdims=True))
    return a - b + lam_init


def _pdiff_kernel(lq1, lk1, lq2, lk2, q_ref, k_ref, v_ref, o_ref, lhs, m_s, l_s, acc,
                  *, tq, tk, lam_init, online):
    i = pl.program_id(0)
    for hg in range(H_A):
        lhs[hg] = _split_components(q_ref[:, hg * LANES:(hg + 1) * LANES])
    m_s[...] = jnp.full(m_s.shape, NEG_BIG, F32)
    l_s[...] = jnp.zeros(l_s.shape, F32)
    acc[...] = jnp.zeros(acc.shape, F32)

    def tile(j, masked):
        start = pl.multiple_of(j * tk, tk)
        for kv in range(KV_A):
            kt = k_ref[pl.ds(start, tk), kv * LANES:(kv + 1) * LANES]
            vt = v_ref[pl.ds(start, tk), kv * LANES:(kv + 1) * LANES]
            for g in range(G_A):
                hg = kv * G_A + g
                s = _dot_t(lhs[hg], kt)
                keep = None
                if masked:
                    r = i * tq + lax.broadcasted_iota(I32, s.shape, 0) % tq
                    c = j * tk + lax.broadcasted_iota(I32, s.shape, 1)
                    keep = c <= r
                if online:
                    if masked:
                        s = jnp.where(keep, s, -jnp.inf)
                    _online_softmax_step(s, vt, m_s, l_s, acc, hg)
                else:
                    _bounded_softmax_step(s, keep, vt, l_s, acc, hg)

    def body(j, carry):
        tile(j, False)
        return carry

    n_full = (i * tq) // tk
    lax.fori_loop(0, n_full, body, 0)
    tile(n_full, True)
    lam = _lambda(lq1, lk1, lq2, lk2, lam_init)
    for hg in range(H_A):
        o = acc[hg] / _row_sum(l_s[hg], online)
        o_ref[:, hg * LANES:(hg + 1) * LANES] = o[:tq] - lam * o[tq:]


def _prompt_diff_attn(lams, qa, ka, va, lam_init, tq, tk, online):
    t = qa.shape[0]
    assert tk % tq == 0
    kern = functools.partial(_pdiff_kernel, tq=tq, tk=tk, lam_init=lam_init, online=online)
    return pl.pallas_call(
        kern,
        out_shape=jax.ShapeDtypeStruct((t, H_A * DV_A), F32),
        grid=(t // tq,),
        in_specs=[_const_spec((1, HD_A))] * 4 + [pl.BlockSpec((tq, W_QA), lambda i: (i, 0)),
                                                  _const_spec(ka.shape), _const_spec(va.shape)],
        out_specs=pl.BlockSpec((tq, H_A * DV_A), lambda i: (i, 0)),
        scratch_shapes=[pltpu.VMEM((H_A, 2 * tq, LANES), BF16), pltpu.VMEM((H_A, 2 * tq, LANES), F32),
                        pltpu.VMEM((H_A, 2 * tq, LANES), F32), pltpu.VMEM((H_A, 2 * tq, LANES), F32)],
        compiler_params=_cparams(("parallel",)),
        name="prompt_diff_attn",
    )(*lams, qa, ka, va)


def _sortable_key(score):
    bits = pltpu.bitcast(score, I32)
    return bits ^ ((bits >> 31) & 0x7FFFFFFF)


I16_MIN, I16_MAX = -(2 ** 15), 2 ** 15 - 1


def _kth_largest_half(halves, nt, rows, tk, topk):
    width = min(tk, 2 * LANES)

    def count_ge(cand):
        c16 = cand.astype(I16)

        def body(j, cnt):
            hit = jnp.where(halves[j] >= c16, jnp.int16(1), jnp.int16(0))
            for c in range(tk // width):
                cnt = cnt + hit[:, c * width:(c + 1) * width]
            return cnt

        cnt = lax.fori_loop(0, nt, body, jnp.zeros((rows, width), I16))
        return jnp.sum(cnt.astype(I32), axis=1, keepdims=True)

    zero = jnp.zeros((rows, 1), I32)
    t = jnp.where(count_ge(zero) >= topk, zero, jnp.full((rows, 1), I16_MIN, I32))

    def body(b, t):
        cand = t | jnp.left_shift(jnp.int32(1), 14 - b)
        return jnp.where(count_ge(cand) >= topk, cand, t)

    return lax.fori_loop(0, 15, body, t)


def _topk_threshold(keys, hi16, lo16, nt, rows, tk, topk):
    top = _kth_largest_half(hi16, nt, rows, tk, topk)

    def low_tile(j, carry):
        key = keys[j]
        hi = key >> 16
        lo = (key & 0xFFFF) + I16_MIN
        lo16[j] = jnp.where(hi == top, lo, jnp.where(hi > top, I16_MAX, I16_MIN)).astype(I16)
        return carry

    lax.fori_loop(0, nt, low_tile, 0)
    low = _kth_largest_half(lo16, nt, rows, tk, topk)
    return jnp.maximum(top * 65536 + (low - I16_MIN), INT_MIN + 1)


def _pdsa_kernel(qi_ref, wi_ref, qb_ref, ki_ref, kb_ref, vb_ref, o_ref, keys, hi16, lo16, m_s, l_s, acc,
                 *, tq, tk, topk, online):
    i = pl.program_id(0)
    nt = (i * tq + tq + tk - 1) // tk
    row = i * tq + lax.broadcasted_iota(I32, (tq, tk), 0)
    col0 = lax.broadcasted_iota(I32, (tq, tk), 1)

    qi_st = jnp.concatenate([qi_ref[:, h * D_IDX:(h + 1) * D_IDX] for h in range(H_IDX)], axis=0)
    wv = wi_ref[...]
    w_cols = [wv[:, D_IDX + h:D_IDX + h + 1] for h in range(H_IDX)]

    def score_tile(j):
        start = pl.multiple_of(j * tk, tk)
        lg = _dot_t(qi_st, ki_ref[pl.ds(start, tk), :])
        sc = jnp.zeros((tq, tk), F32)
        for h in range(H_IDX):
            sc = sc + jnp.maximum(lg[h * tq:(h + 1) * tq], 0.0) * w_cols[h]
        key = jnp.where(col0 + j * tk <= row, _sortable_key(sc), INT_MIN)
        keys[j] = key
        hi16[j] = (key >> 16).astype(I16)

    _for_each_tile(nt, score_tile)

    thr = _topk_threshold(keys, hi16, lo16, nt, tq, tk, topk)

    m_s[...] = jnp.full(m_s.shape, NEG_BIG, F32)
    l_s[...] = jnp.zeros(l_s.shape, F32)
    acc[...] = jnp.zeros(acc.shape, F32)
    q_st = [jnp.concatenate([qb_ref[:, (kv * G_B + g) * HD_B:(kv * G_B + g + 1) * HD_B] for g in range(G_B)], axis=0)
            for kv in range(KV_B)]

    def attn_tile(j):
        start = pl.multiple_of(j * tk, tk)
        kt = keys[j]
        sel = kt >= thr
        sel = jnp.concatenate([sel] * G_B, axis=0)
        for kv in range(KV_B):
            s = _dot_t(q_st[kv], kb_ref[pl.ds(start, tk), kv * HD_B:(kv + 1) * HD_B])
            vt = vb_ref[pl.ds(start, tk), kv * HD_B:(kv + 1) * HD_B]
            if online:
                _online_softmax_step(jnp.where(sel, s, -jnp.inf), vt, m_s, l_s, acc, kv)
            else:
                _bounded_softmax_step(s, sel, vt, l_s, acc, kv)

    _for_each_tile(nt, attn_tile)
    for kv in range(KV_B):
        o = acc[kv] / _row_sum(l_s[kv], online)
        for g in range(G_B):
            hg = kv * G_B + g
            o_ref[:, hg * HD_B:(hg + 1) * HD_B] = o[g * tq:(g + 1) * tq]


def _prompt_dsa(qi, wi, qb, ki, kb, vb, tq, tk, online):
    t = qb.shape[0]
    topk = min(TOPK_MAX, t // 4)
    kern = functools.partial(_pdsa_kernel, tq=tq, tk=tk, topk=topk, online=online)
    row = lambda w: pl.BlockSpec((tq, w), lambda i: (i, 0))
    return pl.pallas_call(
        kern,
        out_shape=jax.ShapeDtypeStruct((t, H_B * HD_B), F32),
        grid=(t // tq,),
        in_specs=[row(W_QI), row(LANES), row(W_QB), _const_spec(ki.shape), _const_spec(kb.shape),
                  _const_spec(vb.shape)],
        out_specs=row(H_B * HD_B),
        scratch_shapes=[pltpu.VMEM((t // tk, tq, tk), I32), pltpu.VMEM((t // tk, tq, tk), I16),
                        pltpu.VMEM((t // tk, tq, tk), I16), pltpu.VMEM((KV_B, G_B * tq, LANES), F32),
                        pltpu.VMEM((KV_B, G_B * tq, LANES), F32), pltpu.VMEM((KV_B, G_B * tq, LANES), F32)],
        compiler_params=_cparams(("parallel",)),
        name="prompt_dsa",
    )(qi, wi, qb, ki, kb, vb)


def _fetch_pages(pt_ref, pools, bufs, sems, *, pps, nsteps, total):
    t = pl.program_id(0) * nsteps + pl.program_id(1)
    slot = t % 2

    def copies(tt, s, p):
        pg = pt_ref[tt // nsteps, (tt % nsteps) * pps + p]
        return [pltpu.make_async_copy(pool.at[pg], buf.at[s, p], sem.at[s])
                for pool, buf, sem in zip(pools, bufs, sems)]

    def start(tt, s):
        def body(p, carry):
            for cp in copies(tt, s, p):
                cp.start()
            return carry
        lax.fori_loop(0, pps, body, 0)

    @pl.when(t == 0)
    def _():
        start(t, slot)

    @pl.when(t + 1 < total)
    def _():
        start(t + 1, 1 - slot)

    def wait(p, carry):
        for cp in copies(t, slot, p):
            cp.wait()
        return carry

    lax.fori_loop(0, pps, wait, 0)
    return slot


def _page_scratch(pools, pps):
    bufs = [pltpu.VMEM((2, pps) + pool.shape[1:], pool.dtype) for pool in pools]
    sems = [pltpu.SemaphoreType.DMA((2,)) for _ in pools]
    return bufs + sems


def _sdiff_kernel(pt_ref, lq1, lk1, lq2, lk2, q_ref, kn_ref, vn_ref, kpool, vpool, o_ref,
                  kbuf, vbuf, ksem, vsem, kc, vc, lhs, m_s, l_s, acc, *, pps, page, tn, lam_init, nsteps, total):
    j = pl.program_id(1)

    @pl.when(j == 0)
    def _():
        q = q_ref[0].astype(F32)
        for kv in range(KV_A):
            parts = [_split_components(q[:, (kv * G_A + g) * LANES:(kv * G_A + g + 1) * LANES]) for g in range(G_A)]
            lhs[kv] = jnp.concatenate(parts, axis=0).astype(BF16)
        m_s[...] = jnp.full(m_s.shape, NEG_BIG, F32)
        l_s[...] = jnp.zeros(l_s.shape, F32)
        acc[...] = jnp.zeros(acc.shape, F32)

    slot = _fetch_pages(pt_ref, [kpool, vpool], [kbuf, vbuf], [ksem, vsem], pps=pps, nsteps=nsteps, total=total)
    for p in range(pps):
        for kv in range(KV_A):
            kc[kv, :, p * page:(p + 1) * page] = kbuf[slot, p, kv].reshape(2 * HD_A, page).astype(BF16)
            vc[kv, p * page:(p + 1) * page, :] = vbuf[slot, p, pl.ds(kv, page, stride=KV_A), :].astype(BF16)
    for kv in range(KV_A):
        s = _dot(lhs[kv], kc[kv])
        _online_softmax_step(s, vc[kv], m_s, l_s, acc, kv)

    @pl.when(j == pl.num_programs(1) - 1)
    def _():
        kn = kn_ref[0]
        vn = vn_ref[0]
        lam = _lambda(lq1, lk1, lq2, lk2, lam_init)
        for kv in range(KV_A):
            s = _dot_t(lhs[kv], kn[:, kv * LANES:(kv + 1) * LANES])
            r = lax.broadcasted_iota(I32, s.shape, 0) % tn
            c = lax.broadcasted_iota(I32, s.shape, 1)
            s = jnp.where(c <= r, s, -jnp.inf)
            _online_softmax_step(s, vn[:, kv * LANES:(kv + 1) * LANES], m_s, l_s, acc, kv)
            o = acc[kv] / l_s[kv]
            for g in range(G_A):
                hg = kv * G_A + g
                o_ref[0, :, hg * LANES:(hg + 1) * LANES] = (o[(2 * g) * tn:(2 * g + 1) * tn]
                                                            - lam * o[(2 * g + 1) * tn:(2 * g + 2) * tn])


def _sample_diff_attn(page_table, lams, qa, ka_new, va_new, pool_kt, pool_v, lam_init, pps):
    nb, tn, _ = qa.shape
    n_pages = page_table.shape[1]
    page = pool_kt.shape[-1]
    nsteps = n_pages // pps
    kern = functools.partial(_sdiff_kernel, pps=pps, page=page, tn=tn, lam_init=lam_init, nsteps=nsteps,
                             total=nb * nsteps)
    seq = lambda w: pl.BlockSpec((1, tn, w), lambda b, j, pt: (b, 0, 0))
    cst = lambda shape: pl.BlockSpec(shape, lambda b, j, pt: (0,) * len(shape))
    hbm = pl.BlockSpec(memory_space=pl.ANY)
    rows = 2 * G_A * tn
    return pl.pallas_call(
        kern,
        out_shape=jax.ShapeDtypeStruct((nb, tn, H_A * DV_A), F32),
        grid_spec=pltpu.PrefetchScalarGridSpec(
            num_scalar_prefetch=1,
            grid=(nb, nsteps),
            in_specs=[cst((1, HD_A))] * 4 + [seq(W_QA), seq(W_KA), seq(W_VA), hbm, hbm],
            out_specs=seq(H_A * DV_A),
            scratch_shapes=_page_scratch([pool_kt, pool_v], pps)
                           + [pltpu.VMEM((KV_A, 2 * HD_A, pps * page), BF16),
                              pltpu.VMEM((KV_A, pps * page, DV_A), BF16),
                              pltpu.VMEM((KV_A, rows, LANES), BF16), pltpu.VMEM((KV_A, rows, LANES), F32),
                              pltpu.VMEM((KV_A, rows, LANES), F32), pltpu.VMEM((KV_A, rows, LANES), F32)]),
        compiler_params=_cparams(("arbitrary", "arbitrary")),
        name="sample_diff_attn",
    )(page_table, *lams, qa, ka_new, va_new, pool_kt, pool_v)


def _sscore_kernel(pt_ref, qi_ref, wi_ref, kin_ref, ipool, keys_o, newkeys_o, ibuf, isem,
                   *, pps, page, tn, tile, nsteps, total):
    j = pl.program_id(1)
    q = qi_ref[0].astype(F32)
    qi_st = jnp.concatenate([q[:, h * D_IDX:(h + 1) * D_IDX] for h in range(H_IDX)], axis=0).astype(BF16)
    wv = wi_ref[0]
    w_cols = [wv[:, D_IDX + h:D_IDX + h + 1] for h in range(H_IDX)]

    def scores(lg):
        sc = jnp.zeros((tn, lg.shape[1]), F32)
        for h in range(H_IDX):
            sc = sc + jnp.maximum(lg[h * tn:(h + 1) * tn], 0.0) * w_cols[h]
        return _sortable_key(sc)

    slot = _fetch_pages(pt_ref, [ipool], [ibuf], [isem], pps=pps, nsteps=nsteps, total=total)
    kt = jnp.concatenate([ibuf[slot, p] for p in range(pps)], axis=1).astype(BF16)
    keys = scores(_dot(qi_st, kt))
    for c in range(pps * page // tile):
        keys_o[c] = keys[:, c * tile:(c + 1) * tile]

    @pl.when(j == nsteps - 1)
    def _():
        nk = scores(_dot_t(qi_st, kin_ref[0]))
        r = lax.broadcasted_iota(I32, nk.shape, 0)
        c = lax.broadcasted_iota(I32, nk.shape, 1)
        nk = jnp.where(c <= r, nk, INT_MIN)
        newkeys_o[...] = jnp.concatenate([nk, jnp.full((tn, LANES - tn), INT_MIN, I32)], axis=1)


def _sample_scores(page_table, qi, wi, ki_new, pool_it, pps, tile):
    nb, tn, _ = qi.shape
    n_pages = page_table.shape[1]
    page = pool_it.shape[2]
    nsteps = n_pages // pps
    tiles_per_step = pps * page // tile
    kern = functools.partial(_sscore_kernel, pps=pps, page=page, tn=tn, tile=tile, nsteps=nsteps,
                             total=nb * nsteps)
    seq = lambda w: pl.BlockSpec((1, tn, w), lambda b, j, pt: (b, 0, 0))
    return pl.pallas_call(
        kern,
        out_shape=[jax.ShapeDtypeStruct((n_pages * page // tile, nb * tn, tile), I32),
                   jax.ShapeDtypeStruct((nb * tn, LANES), I32)],
        grid_spec=pltpu.PrefetchScalarGridSpec(
            num_scalar_prefetch=1,
            grid=(nb, nsteps),
            in_specs=[seq(W_QI), seq(LANES), seq(D_IDX), pl.BlockSpec(memory_space=pl.ANY)],
            out_specs=[pl.BlockSpec((tiles_per_step, tn, tile), lambda b, j, pt: (j, b, 0)),
                       pl.BlockSpec((tn, LANES), lambda b, j, pt: (b, 0))],
            scratch_shapes=_page_scratch([pool_it], pps)),
        compiler_params=_cparams(("arbitrary", "arbitrary")),
        name="sample_scores",
    )(page_table, qi, wi, ki_new, pool_it)


def _sthr_kernel(keys_ref, newkeys_ref, sel_ref, keys, hi16, lo16, *, rows, tile, nt, topk):
    for j in range(nt - 1):
        keys[j] = keys_ref[j]
    pad = [jnp.full((rows, tile - LANES), INT_MIN, I32)] if tile > LANES else []
    keys[nt - 1] = jnp.concatenate([newkeys_ref[...]] + pad, axis=1)

    def split(j, carry):
        hi16[j] = (keys[j] >> 16).astype(I16)
        return carry

    lax.fori_loop(0, nt, split, 0)
    thr = _topk_threshold(keys, hi16, lo16, nt, rows, tile, topk)
    for j in range(nt - 1):
        sel_ref[:, j * tile:(j + 1) * tile] = (keys[j] >= thr).astype(F32)
    sel_ref[:, (nt - 1) * tile:] = (newkeys_ref[...] >= thr).astype(F32)


def _sample_select(keys, newkeys, topk):
    n_tiles, rows_all, tile = keys.shape
    rows = _tile(rows_all, 128)
    nt = n_tiles + 1
    kern = functools.partial(_sthr_kernel, rows=rows, tile=tile, nt=nt, topk=topk)
    width = n_tiles * tile + LANES
    return pl.pallas_call(
        kern,
        out_shape=jax.ShapeDtypeStruct((rows_all, width), F32),
        grid=(rows_all // rows,),
        in_specs=[pl.BlockSpec((n_tiles, rows, tile), lambda i: (0, i, 0)),
                  pl.BlockSpec((rows, LANES), lambda i: (i, 0))],
        out_specs=pl.BlockSpec((rows, width), lambda i: (i, 0)),
        scratch_shapes=[pltpu.VMEM((nt, rows, tile), I32), pltpu.VMEM((nt, rows, tile), I16),
                        pltpu.VMEM((nt, rows, tile), I16)],
        compiler_params=_cparams(("parallel",)),
        name="sample_select",
    )(keys, newkeys)


def _sdsa_kernel(pt_ref, q_ref, sel_ref, seln_ref, kn_ref, vn_ref, kpool, vpool, o_ref,
                 kbuf, vbuf, ksem, vsem, kc, vc, lhs, m_s, l_s, acc, *, pps, page, tn, nsteps, total):
    j = pl.program_id(1)

    @pl.when(j == 0)
    def _():
        q = q_ref[0].astype(F32)
        for kv in range(KV_B):
            parts = [q[:, (kv * G_B + g) * HD_B:(kv * G_B + g + 1) * HD_B] for g in range(G_B)]
            lhs[kv] = jnp.concatenate(parts, axis=0).astype(BF16)
        m_s[...] = jnp.full(m_s.shape, NEG_BIG, F32)
        l_s[...] = jnp.zeros(l_s.shape, F32)
        acc[...] = jnp.zeros(acc.shape, F32)

    def step(sel, kmats, vmats):
        sel = jnp.concatenate([sel > 0.5] * G_B, axis=0)
        for kv in range(KV_B):
            s = jnp.where(sel, _dot_t(lhs[kv], kmats[kv]), -jnp.inf)
            _online_softmax_step(s, vmats[kv], m_s, l_s, acc, kv)

    slot = _fetch_pages(pt_ref, [kpool, vpool], [kbuf, vbuf], [ksem, vsem], pps=pps, nsteps=nsteps, total=total)
    for p in range(pps):
        for kv in range(KV_B):
            kc[kv, p * page:(p + 1) * page, :] = kbuf[slot, p, pl.ds(kv, page, stride=KV_B), :].astype(BF16)
            vc[kv, p * page:(p + 1) * page, :] = vbuf[slot, p, pl.ds(kv, page, stride=KV_B), :].astype(BF16)
    step(sel_ref[0], [kc[kv] for kv in range(KV_B)], [vc[kv] for kv in range(KV_B)])

    @pl.when(j == nsteps - 1)
    def _():
        kn, vn = kn_ref[0], vn_ref[0]
        step(seln_ref[0][:, :tn], [kn[:, kv * HD_B:(kv + 1) * HD_B] for kv in range(KV_B)],
             [vn[:, kv * HD_B:(kv + 1) * HD_B] for kv in range(KV_B)])
        for kv in range(KV_B):
            o = acc[kv] / l_s[kv]
            for g in range(G_B):
                hg = kv * G_B + g
                o_ref[0, :, hg * HD_B:(hg + 1) * HD_B] = o[g * tn:(g + 1) * tn]


def _sample_dsa(page_table, qb, sel, kb_new, vb_new, pool_k, pool_v, pps):
    nb, tn, _ = qb.shape
    n_pages = page_table.shape[1]
    page = pool_k.shape[1] // KV_B
    tk = pps * page
    nsteps = n_pages // pps
    kern = functools.partial(_sdsa_kernel, pps=pps, page=page, tn=tn, nsteps=nsteps, total=nb * nsteps)
    seq = lambda w: pl.BlockSpec((1, tn, w), lambda b, j, pt: (b, 0, 0))
    hbm = pl.BlockSpec(memory_space=pl.ANY)
    rows = G_B * tn
    return pl.pallas_call(
        kern,
        out_shape=jax.ShapeDtypeStruct((nb, tn, H_B * HD_B), F32),
        grid_spec=pltpu.PrefetchScalarGridSpec(
            num_scalar_prefetch=1,
            grid=(nb, nsteps),
            in_specs=[seq(W_QB), pl.BlockSpec((1, tn, tk), lambda b, j, pt: (b, 0, j)),
                      pl.BlockSpec((1, tn, LANES), lambda b, j, pt: (b, 0, (n_pages * page) // LANES)),
                      seq(W_KB), seq(W_KB), hbm, hbm],
            out_specs=seq(H_B * HD_B),
            scratch_shapes=_page_scratch([pool_k, pool_v], pps)
                           + [pltpu.VMEM((KV_B, tk, HD_B), BF16), pltpu.VMEM((KV_B, tk, HD_B), BF16),
                              pltpu.VMEM((KV_B, rows, LANES), BF16), pltpu.VMEM((KV_B, rows, LANES), F32),
                              pltpu.VMEM((KV_B, rows, LANES), F32), pltpu.VMEM((KV_B, rows, LANES), F32)]),
        compiler_params=_cparams(("arbitrary", "arbitrary")),
        name="sample_dsa",
    )(page_table, qb, sel, sel, kb_new, vb_new, pool_k, pool_v)


def _finish_kernel(x_ref, oa_ref, ob_ref, ga_ref, gb_ref, sub_ref, gm_ref, wa, wb, wo, wu, wd, y_ref, *, lam_init):
    oa = oa_ref[...]
    sub = sub_ref[...]
    chunks = []
    for h in range(H_A):
        c = oa[:, h * DV_A:(h + 1) * DV_A]
        ms = jnp.mean(c * c, axis=-1, keepdims=True)
        chunks.append(((c * lax.rsqrt(ms + EPS)) * sub) * (1.0 - lam_init))
    oan = jnp.concatenate(chunks, axis=1).astype(BF16)
    pa = _dot(oan, wa[...])
    pb = _dot(ob_ref[...].astype(BF16), wb[...])
    mix = (ga_ref[...] * pa + gb_ref[...] * pb).astype(BF16)
    x1 = x_ref[...] + _dot(mix, wo[...])
    ms = jnp.mean(x1 * x1, axis=-1, keepdims=True)
    xn = ((x1 * lax.rsqrt(ms + EPS)) * gm_ref[...]).astype(BF16)
    hid = jnp.maximum(_dot(xn, wu[...]), 0.0)
    y_ref[...] = x1 + _dot((hid * hid).astype(BF16), wd[...])


def _finish(x, oa, ob, ga, gb, pw, lam_init, tm):
    t, d = x.shape
    row = pl.BlockSpec((tm, d), lambda i: (i, 0))
    weights = [pw[k] for k in ("w_a_proj", "w_b_proj", "w_out", "w_up", "w_down")]
    kern = functools.partial(_finish_kernel, lam_init=lam_init)
    return pl.pallas_call(
        kern,
        out_shape=jax.ShapeDtypeStruct((t, d), F32),
        grid=(t // tm,),
        in_specs=[row] * 5 + [_const_spec((1, DV_A)), _const_spec((1, d))] + [_const_spec(w.shape) for w in weights],
        out_specs=row,
        compiler_params=_cparams(("parallel",)),
        name="finish",
    )(x, oa, ob, ga, gb, pw["sub_norm_a"], pw["g_mlp"], *weights)


def _prep_weights(l, g_attn_norm, w_in, q_norm_a, k_norm_a, sub_norm_a, q_norm_b, k_norm_b, idx_k_norm,
                  w_a_proj, w_b_proj, w_out, g_mlp_norm, w_up, w_down):
    w = w_in[l].astype(BF16)
    widths = (W_QA, W_KA, W_VA, W_QB, W_KB, W_KB, W_QI, D_IDX, H_IDX, w.shape[0], w.shape[0])
    offs = np.concatenate([[0], np.cumsum(widths)])
    seg = [w[:, int(offs[i]):int(offs[i + 1])] for i in range(len(widths))]
    wkw = jnp.concatenate([seg[7], seg[8], jnp.zeros((w.shape[0], LANES - D_IDX - H_IDX), BF16)], axis=1)
    lane = np.arange(LANES)
    row2 = lambda v: v.astype(F32)[None, :]
    return {
        "g_attn": row2(g_attn_norm[l]),
        "wqa": seg[0], "wka": seg[1], "wva": seg[2], "wqb": seg[3], "wkb": seg[4], "wvb": seg[5],
        "wqi": seg[6], "wkw": wkw, "wga": seg[9], "wgb": seg[10],
        "gqa": row2(jnp.tile(q_norm_a[l], W_QA // HD_A) * (HD_A ** -0.5 * LOG2E)),
        "gka": row2(jnp.tile(k_norm_a[l], W_KA // HD_A)),
        "gqb": row2(jnp.tile(q_norm_b[l], W_QB // HD_B) * (HD_B ** -0.5 * LOG2E)),
        "gkb": row2(jnp.tile(k_norm_b[l], W_KB // HD_B)),
        "gki": row2(jnp.concatenate([idx_k_norm[l], jnp.zeros((LANES - D_IDX,), F32)])),
        "ones64": jnp.asarray((lane[:, None] // HD_A) == (lane[None, :] // HD_A), BF16),
        "ones128": jnp.ones((LANES, LANES), BF16),
        "sub_norm_a": row2(sub_norm_a[l]), "g_mlp": row2(g_mlp_norm[l]),
        "w_a_proj": w_a_proj[l].astype(BF16), "w_b_proj": w_b_proj[l].astype(BF16),
        "w_out": w_out[l].astype(BF16), "w_up": w_up[l].astype(BF16), "w_down": w_down[l].astype(BF16),
    }


def _tile(n, pref):
    return pref if n % pref == 0 else n


def _tiles(seq, n_sample_rows, n_pages):
    pps = next(p for p in (32, 16, 8, 4, 2, 1) if n_pages % p == 0)
    return {"proj": _tile(seq, 256), "diff": _tile(seq, 256), "diff_k": _tile(seq, 512),
            "dsa_q": _tile(seq, 128), "dsa_k": _tile(seq, 512),
            "finish": _tile(seq, 256), "sample_rows": _tile(n_sample_rows, 256), "pps": pps,
            "sel_tile": _tile(pps * 128, 512)}


def _score_bound(q_gain, k_gain, head_dim):
    return (head_dim ** 0.5) * jnp.max(jnp.abs(q_gain)) * jnp.max(jnp.abs(k_gain))


def kernel(x_prompt, x_sample, cache_a_k, cache_a_v, cache_b_k, cache_b_v, cache_b_idx_k, page_table, g_attn_norm, w_in, q_norm_a, k_norm_a, lam_q1, lam_k1, lam_q2, lam_k2, sub_norm_a, q_norm_b, k_norm_b, idx_k_norm, w_a_proj, w_b_proj, w_out, g_mlp_norm, w_up, w_down):
    depth = w_in.shape[0]
    nb_p, seq, d = x_prompt.shape
    nb_s, tn, _ = x_sample.shape
    n_pages = page_table.shape[1]
    n_phys, page = cache_a_k.shape[1], cache_a_k.shape[2]
    past = n_pages * page
    assert nb_p == 1
    pos_p = jnp.arange(seq)
    pos_s = past + (jnp.arange(nb_s * tn) % tn)
    tl = _tiles(seq, nb_s * tn, n_pages)
    pps = tl["pps"]

    xp = x_prompt.reshape(seq, d)
    xs = x_sample.reshape(nb_s * tn, d)
    outs_p = [[] for _ in range(5)]
    outs_s = [[] for _ in range(5)]
    for l in range(depth):
        lam_init = 0.8 - 0.6 * math.exp(-0.3 * l)
        pw = _prep_weights(l, g_attn_norm, w_in, q_norm_a, k_norm_a, sub_norm_a, q_norm_b, k_norm_b, idx_k_norm,
                           w_a_proj, w_b_proj, w_out, g_mlp_norm, w_up, w_down)
        lams = [v[l].astype(F32)[None, :] for v in (lam_q1, lam_k1, lam_q2, lam_k2)]

        (qa, kaf, kab, vaf, vab, qb, kbf, kbb, vbf, vbb, qi, kif, kib, wi, ga, gb) = _project(
            xs, pos_s, pw, tl["sample_rows"])
        seq3 = lambda a: a.reshape(nb_s, tn, a.shape[-1])
        pool_akt = jnp.transpose(cache_a_k[l], (0, 2, 3, 4, 1))
        pool_it = jnp.transpose(cache_b_idx_k[l], (0, 2, 1))
        rows2 = lambda c: c[l].reshape(n_phys, page * c.shape[3], c.shape[4])
        oa = _sample_diff_attn(page_table, lams, seq3(qa), seq3(kab), seq3(vab), pool_akt, rows2(cache_a_v),
                               lam_init, pps)
        keys, newkeys = _sample_scores(page_table, seq3(qi), seq3(wi), seq3(kib), pool_it, pps, tl["sel_tile"])
        sel = seq3(_sample_select(keys, newkeys, min(TOPK_MAX, (past + tn) // 4)))
        ob = _sample_dsa(page_table, seq3(qb), sel, seq3(kbb), seq3(vbb), rows2(cache_b_k), rows2(cache_b_v), pps)
        xs = _finish(xs, oa.reshape(nb_s * tn, -1), ob.reshape(nb_s * tn, -1), ga, gb, pw, lam_init,
                     tl["sample_rows"])
        for lst, v in zip(outs_s, (kaf.reshape(nb_s, tn, KV_A, 2, HD_A), vaf.reshape(nb_s, tn, KV_A, DV_A),
                                   kbf.reshape(nb_s, tn, KV_B, HD_B), vbf.reshape(nb_s, tn, KV_B, HD_B),
                                   kif.reshape(nb_s, tn, D_IDX))):
            lst.append(v)

        (qa, kaf, kab, vaf, vab, qb, kbf, kbb, vbf, vbb, qi, kif, kib, wi, ga, gb) = _project(
            xp, pos_p, pw, tl["proj"])
        diff = lambda online: functools.partial(_prompt_diff_attn, lams, qa, kab, vab, lam_init, tl["diff"], tl["diff_k"],
                                                online)
        oa = lax.cond(_score_bound(q_norm_a[l], k_norm_a[l], HD_A) <= MAX_BOUNDED_SCORE, diff(False), diff(True))
        dsa = lambda online: functools.partial(_prompt_dsa, qi, wi, qb, kib, kbb, vbb, tl["dsa_q"], tl["dsa_k"],
                                               online)
        ob = lax.cond(_score_bound(q_norm_b[l], k_norm_b[l], HD_B) <= MAX_BOUNDED_SCORE, dsa(False), dsa(True))
        xp = _finish(xp, oa, ob, ga, gb, pw, lam_init, tl["finish"])
        for lst, v in zip(outs_p, (kaf.reshape(1, seq, KV_A, 2, HD_A), vaf.reshape(1, seq, KV_A, DV_A),
                                   kbf.reshape(1, seq, KV_B, HD_B), vbf.reshape(1, seq, KV_B, HD_B),
                                   kif.reshape(1, seq, D_IDX))):
            lst.append(v)

    return (xp.reshape(nb_p, seq, d), xs.reshape(nb_s, tn, d),
            *[jnp.stack(v) for v in outs_p], *[jnp.stack(v) for v in outs_s])
```

```python
import functools
import math

import jax
import jax.numpy as jnp
import numpy as np
from jax import lax
from jax.experimental import pallas as pl
from jax.experimental.pallas import tpu as pltpu

F32 = jnp.float32
BF16 = jnp.bfloat16
I32 = jnp.int32
I16 = jnp.int16

LANES = 128
VMEM_LIMIT = 56 * 1024 * 1024
EPS = 1e-6
ROPE_THETA = 10000.0
TOPK_MAX = 256
NEG_BIG = -1e30
INT_MIN = -(2 ** 31)
LOG2E = 1.4426950408889634
MAX_BOUNDED_SCORE = 60.0

H_A, KV_A, HD_A = 8, 2, 64
G_A = H_A // KV_A
DV_A = 2 * HD_A
H_B, KV_B, HD_B = 8, 2, 128
G_B = H_B // KV_B
H_IDX, D_IDX = 8, 64
W_QA = H_A * 2 * HD_A
W_KA = KV_A * 2 * HD_A
W_VA = KV_A * DV_A
W_QB = H_B * HD_B
W_KB = KV_B * HD_B
W_QI = H_IDX * D_IDX


def _cparams(sem):
    return pltpu.CompilerParams(dimension_semantics=sem, vmem_limit_bytes=VMEM_LIMIT)


def _const_spec(shape):
    nd = len(shape)
    return pl.BlockSpec(shape, lambda *_: (0,) * nd, pipeline_mode=pl.Buffered(1))


def _dot_t(a, b):
    return lax.dot_general(a, b, (((1,), (1,)), ((), ())), preferred_element_type=F32)


def _dot(a, b):
    return jnp.dot(a, b, preferred_element_type=F32)


def _group_rms(h, ones_ref, inv_group):
    outs = []
    for c in range(h.shape[1] // LANES):
        hs = h[:, c * LANES:(c + 1) * LANES]
        sq = hs * hs
        hi = sq.astype(BF16)
        lo = (sq - hi.astype(F32)).astype(BF16)
        ss = _dot(hi, ones_ref[...]) + _dot(lo, ones_ref[...])
        outs.append(hs * lax.rsqrt(ss * inv_group + EPS))
    return outs


def _rope_chunks(chunks, gain, cos, sin, half):
    lane = lax.broadcasted_iota(I32, chunks[0].shape, 1)
    first = (lane % (2 * half)) < half
    outs = []
    for c, y in enumerate(chunks):
        y = y * gain[:, c * LANES:(c + 1) * LANES]
        if half == LANES // 2:
            partner = pltpu.roll(y, half, 1)
        else:
            partner = jnp.where(first, pltpu.roll(y, LANES - half, 1), pltpu.roll(y, half, 1))
        outs.append(y * cos + partner * sin)
    return outs


def _proj_kernel(x_ref, g_ref, wqa, wka, wva, wqb, wkb, wvb, wqi, wkw, wga, wgb,
                 gqa, gka, gqb, gkb, gki, cos_a, sin_a, cos_b, sin_b, ones64, ones128,
                 qa_o, kaf_o, kab_o, vaf_o, vab_o, qb_o, kbf_o, kbb_o, vbf_o, vbb_o,
                 qi_o, kif_o, kib_o, wi_o, ga_o, gb_o):
    x = x_ref[...]
    ms = jnp.mean(x * x, axis=-1, keepdims=True)
    xn = ((x * lax.rsqrt(ms + EPS)) * g_ref[...]).astype(BF16)
    ca, sa, cb, sb = cos_a[...], sin_a[...], cos_b[...], sin_b[...]

    def cat(chunks):
        return chunks[0] if len(chunks) == 1 else jnp.concatenate(chunks, axis=1)

    h = _dot(xn, wqa[...])
    qa_o[...] = cat(_rope_chunks(_group_rms(h, ones64, 1.0 / HD_A), gqa[...], ca, sa, HD_A // 2)).astype(BF16)
    h = _dot(xn, wka[...])
    ka = cat(_rope_chunks(_group_rms(h, ones64, 1.0 / HD_A), gka[...], ca, sa, HD_A // 2))
    kaf_o[...] = ka
    kab_o[...] = ka.astype(BF16)
    h = _dot(xn, wva[...])
    vaf_o[...] = h
    vab_o[...] = h.astype(BF16)
    h = _dot(xn, wqb[...])
    qb_o[...] = cat(_rope_chunks(_group_rms(h, ones128, 1.0 / HD_B), gqb[...], cb, sb, HD_B // 2)).astype(BF16)
    h = _dot(xn, wkb[...])
    kb = cat(_rope_chunks(_group_rms(h, ones128, 1.0 / HD_B), gkb[...], cb, sb, HD_B // 2))
    kbf_o[...] = kb
    kbb_o[...] = kb.astype(BF16)
    h = _dot(xn, wvb[...])
    vbf_o[...] = h
    vbb_o[...] = h.astype(BF16)
    h = _dot(xn, wqi[...])
    chunks = [h[:, c * LANES:(c + 1) * LANES] for c in range(W_QI // LANES)]
    ones_gain = jnp.full((1, W_QI), D_IDX ** -0.5, F32)
    qi_o[...] = cat(_rope_chunks(chunks, ones_gain, ca, sa, D_IDX // 2)).astype(BF16)
    h = _dot(xn, wkw[...])
    ki = _rope_chunks(_group_rms(h, ones64, 1.0 / D_IDX), gki[...], ca, sa, D_IDX // 2)[0]
    kif_o[...] = ki[:, :D_IDX]
    kib_o[...] = ki[:, :D_IDX].astype(BF16)
    wi_o[...] = h * (H_IDX ** -0.5)
    ga_o[...] = jax.nn.sigmoid(_dot(xn, wga[...]))
    gb_o[...] = jax.nn.sigmoid(_dot(xn, wgb[...]))


def _rope_tables(pos, half, group):
    inv_freq = ROPE_THETA ** (-jnp.arange(half, dtype=F32) / half)
    ang = pos.astype(F32)[:, None] * inv_freq[None, :]
    cos, sin = jnp.cos(ang), jnp.sin(ang)
    reps = LANES // group
    cos_t = jnp.tile(jnp.concatenate([cos, cos], axis=1), (1, reps))
    sin_t = jnp.tile(jnp.concatenate([-sin, sin], axis=1), (1, reps))
    return cos_t, sin_t


def _project(x, pos, pw, tm):
    t, d = x.shape
    cos_a, sin_a = _rope_tables(pos, HD_A // 2, HD_A)
    cos_b, sin_b = _rope_tables(pos, HD_B // 2, HD_B)
    row = lambda w: pl.BlockSpec((tm, w), lambda i: (i, 0))
    weights = [pw[k] for k in ("wqa", "wka", "wva", "wqb", "wkb", "wvb", "wqi", "wkw", "wga", "wgb")]
    consts = [pw[k] for k in ("gqa", "gka", "gqb", "gkb", "gki")]
    in_specs = ([row(d), _const_spec((1, d))] + [_const_spec(w.shape) for w in weights]
                + [_const_spec(c.shape) for c in consts] + [row(LANES)] * 4
                + [_const_spec((LANES, LANES))] * 2)
    outs = [(W_QA, BF16), (W_KA, F32), (W_KA, BF16), (W_VA, F32), (W_VA, BF16),
            (W_QB, BF16), (W_KB, F32), (W_KB, BF16), (W_KB, F32), (W_KB, BF16),
            (W_QI, BF16), (D_IDX, F32), (D_IDX, BF16), (LANES, F32), (d, F32), (d, F32)]
    return pl.pallas_call(
        _proj_kernel,
        out_shape=[jax.ShapeDtypeStruct((t, w), dt) for w, dt in outs],
        grid=(t // tm,),
        in_specs=in_specs,
        out_specs=[row(w) for w, _ in outs],
        compiler_params=_cparams(("parallel",)),
        name="proj",
    )(x, pw["g_attn"], *weights, *consts, cos_a, sin_a, cos_b, sin_b, pw["ones64"], pw["ones128"])


def _lane_rep(col):
    return jnp.broadcast_to(col, (col.shape[0], LANES))


def _online_softmax_step(s, v, m_ref, l_ref, acc_ref, idx):
    m_old = m_ref[idx]
    m_new = jnp.maximum(m_old, _lane_rep(jnp.max(s, axis=1, keepdims=True)))
    alpha = jnp.exp2(m_old - m_new)
    p = jnp.exp2(s - m_new[:, :1])
    l_ref[idx] = alpha * l_ref[idx] + _lane_rep(jnp.sum(p, axis=1, keepdims=True))
    acc_ref[idx] = alpha * acc_ref[idx] + _dot(p.astype(BF16), v)
    m_ref[idx] = m_new


def _bounded_softmax_step(s, keep, v, l_ref, acc_ref, idx):
    p = jnp.exp2(s)
    if keep is not None:
        p = jnp.where(keep, p, 0.0)
    part = p[:, :LANES]
    for c in range(1, p.shape[1] // LANES):
        part = part + p[:, c * LANES:(c + 1) * LANES]
    l_ref[idx] = l_ref[idx] + part
    acc_ref[idx] = acc_ref[idx] + _dot(p.astype(BF16), v)


def _row_sum(l, online):
    return l[:, :1] if online else jnp.sum(l, axis=1, keepdims=True)


TILES_PER_TRIP = 4


def _for_each_tile(nt, fn):
    def group(jj, carry):
        for u in range(TILES_PER_TRIP):
            fn(TILES_PER_TRIP * jj + u)
        return carry

    def single(j, carry):
        fn(j)
        return carry

    lax.fori_loop(0, nt // TILES_PER_TRIP, group, 0)
    lax.fori_loop(nt - nt % TILES_PER_TRIP, nt, single, 0)


def _split_components(qg):
    lane = lax.broadcasted_iota(I32, qg.shape, 1)
    zero = jnp.zeros_like(qg)
    return jnp.concatenate([jnp.where(lane < HD_A, qg, zero), jnp.where(lane >= HD_A, qg, zero)], axis=0)


def _lambda(lq1, lk1, lq2, lk2, lam_init):
    a = jnp.exp(jnp.sum(lq1[...] * lk1[...], axis=1, keepdims=True))
    b = jnp.exp(jnp.sum(lq2[...] * lk2[...], axis=1, keepdims=True))
    return a - b + lam_init


def _pdiff_kernel(lq1, lk1, lq2, lk2, q_ref, k_ref, v_ref, o_ref, lhs, m_s, l_s, acc,
                  *, tq, tk, lam_init, online):
    i = pl.program_id(0)
    for hg in range(H_A):
        lhs[hg] = _split_components(q_ref[:, hg * LANES:(hg + 1) * LANES])
    m_s[...] = jnp.full(m_s.shape, NEG_BIG, F32)
    l_s[...] = jnp.zeros(l_s.shape, F32)
    acc[...] = jnp.zeros(acc.shape, F32)

    def tile(j, masked):
        start = pl.multiple_of(j * tk, tk)
        for kv in range(KV_A):
            kt = k_ref[pl.ds(start, tk), kv * LANES:(kv + 1) * LANES]
            vt = v_ref[pl.ds(start, tk), kv * LANES:(kv + 1) * LANES]
            for g in range(G_A):
                hg = kv * G_A + g
                s = _dot_t(lhs[hg], kt)
                keep = None
                if masked:
                    r = i * tq + lax.broadcasted_iota(I32, s.shape, 0) % tq
                    c = j * tk + lax.broadcasted_iota(I32, s.shape, 1)
                    keep = c <= r
                if online:
                    if masked:
                        s = jnp.where(keep, s, -jnp.inf)
                    _online_softmax_step(s, vt, m_s, l_s, acc, hg)
                else:
                    _bounded_softmax_step(s, keep, vt, l_s, acc, hg)

    def body(j, carry):
        tile(j, False)
        return carry

    n_full = (i * tq) // tk
    lax.fori_loop(0, n_full, body, 0)
    tile(n_full, True)
    lam = _lambda(lq1, lk1, lq2, lk2, lam_init)
    for hg in range(H_A):
        o = acc[hg] / _row_sum(l_s[hg], online)
        o_ref[:, hg * LANES:(hg + 1) * LANES] = o[:tq] - lam * o[tq:]


def _prompt_diff_attn(lams, qa, ka, va, lam_init, tq, tk, online):
    t = qa.shape[0]
    assert tk % tq == 0
    kern = functools.partial(_pdiff_kernel, tq=tq, tk=tk, lam_init=lam_init, online=online)
    return pl.pallas_call(
        kern,
        out_shape=jax.ShapeDtypeStruct((t, H_A * DV_A), F32),
        grid=(t // tq,),
        in_specs=[_const_spec((1, HD_A))] * 4 + [pl.BlockSpec((tq, W_QA), lambda i: (i, 0)),
                                                  _const_spec(ka.shape), _const_spec(va.shape)],
        out_specs=pl.BlockSpec((tq, H_A * DV_A), lambda i: (i, 0)),
        scratch_shapes=[pltpu.VMEM((H_A, 2 * tq, LANES), BF16), pltpu.VMEM((H_A, 2 * tq, LANES), F32),
                        pltpu.VMEM((H_A, 2 * tq, LANES), F32), pltpu.VMEM((H_A, 2 * tq, LANES), F32)],
        compiler_params=_cparams(("parallel",)),
        name="prompt_diff_attn",
    )(*lams, qa, ka, va)


def _sortable_key(score):
    bits = pltpu.bitcast(score, I32)
    return bits ^ ((bits >> 31) & 0x7FFFFFFF)


I16_MIN, I16_MAX = -(2 ** 15), 2 ** 15 - 1


def _kth_largest_half(halves, nt, rows, tk, topk):
    width = min(tk, 2 * LANES)

    def count_ge(cand):
        c16 = cand.astype(I16)

        def body(j, cnt):
            hit = jnp.where(halves[j] >= c16, jnp.int16(1), jnp.int16(0))
            for c in range(tk // width):
                cnt = cnt + hit[:, c * width:(c + 1) * width]
            return cnt

        cnt = lax.fori_loop(0, nt, body, jnp.zeros((rows, width), I16))
        return jnp.sum(cnt.astype(I32), axis=1, keepdims=True)

    zero = jnp.zeros((rows, 1), I32)
    n_zero = count_ge(zero)
    t = jnp.where(n_zero >= topk, zero, jnp.full((rows, 1), I16_MIN, I32))
    n_t = jnp.where(n_zero >= topk, n_zero, nt * tk)

    def body(b, carry):
        t, n_t = carry
        cand = t | jnp.left_shift(jnp.int32(1), 14 - b)
        n = count_ge(cand)
        return jnp.where(n >= topk, cand, t), jnp.where(n >= topk, n, n_t)

    return lax.fori_loop(0, 15, body, (t, n_t))


def _drop_late_ties(keys, nt, rows, tk, thr, excess):
    col0 = lax.broadcasted_iota(I32, (rows, tk), 1)

    def count(pred):
        def body(j, cnt):
            hit = pred(keys[j], col0 + j * tk).astype(I32)
            for c in range(tk // LANES):
                cnt = cnt + hit[:, c * LANES:(c + 1) * LANES]
            return cnt
        return jnp.sum(lax.fori_loop(0, nt, body, jnp.zeros((rows, LANES), I32)), axis=1, keepdims=True)

    keep = count(lambda k, c: k == thr) - excess
    nbits = max(1, (keys.shape[0] * tk - 1).bit_length())

    def body(b, last):
        cand = last | jnp.left_shift(jnp.int32(1), nbits - 1 - b)
        before = count(lambda k, c: (k == thr) & (c < cand))
        return jnp.where(before <= keep - 1, cand, last)

    last = lax.fori_loop(0, nbits, body, jnp.zeros((rows, 1), I32))

    def drop(j, carry):
        k = keys[j]
        keys[j] = jnp.where((k == thr) & (col0 + j * tk > last), INT_MIN, k)
        return carry

    lax.fori_loop(0, nt, drop, 0)


def _topk_threshold(keys, hi16, lo16, nt, rows, tk, topk):
    top, n_top = _kth_largest_half(hi16, nt, rows, tk, topk)

    def low_tile(j, carry):
        key = keys[j]
        hi = key >> 16
        lo = (key & 0xFFFF) + I16_MIN
        lo16[j] = jnp.where(hi == top, lo, jnp.where(hi > top, I16_MAX, I16_MIN)).astype(I16)
        return carry

    lax.fori_loop(0, nt, low_tile, 0)
    low, n_low = _kth_largest_half(lo16, nt, rows, tk, topk)
    raw = top * 65536 + (low - I16_MIN)
    thr = jnp.maximum(raw, INT_MIN + 1)
    n_ge = jnp.where(low == I16_MIN, n_top, n_low)
    excess = jnp.where(raw == INT_MIN, 0, n_ge - topk)

    @pl.when(jnp.max(excess) > 0)
    def _():
        _drop_late_ties(keys, nt, rows, tk, thr, excess)

    return thr


def _pdsa_kernel(qi_ref, wi_ref, qb_ref, ki_ref, kb_ref, vb_ref, o_ref, keys, hi16, lo16, m_s, l_s, acc,
                 *, tq, tk, topk, online):
    i = pl.program_id(0)
    nt = (i * tq + tq + tk - 1) // tk
    row = i * tq + lax.broadcasted_iota(I32, (tq, tk), 0)
    col0 = lax.broadcasted_iota(I32, (tq, tk), 1)

    qi_st = jnp.concatenate([qi_ref[:, h * D_IDX:(h + 1) * D_IDX] for h in range(H_IDX)], axis=0)
    wv = wi_ref[...]
    w_cols = [wv[:, D_IDX + h:D_IDX + h + 1] for h in range(H_IDX)]

    def score_tile(j):
        start = pl.multiple_of(j * tk, tk)
        lg = _dot_t(qi_st, ki_ref[pl.ds(start, tk), :])
        sc = jnp.zeros((tq, tk), F32)
        for h in range(H_IDX):
            sc = sc + jnp.maximum(lg[h * tq:(h + 1) * tq], 0.0) * w_cols[h]
        key = jnp.where(col0 + j * tk <= row, _sortable_key(sc), INT_MIN)
        keys[j] = key
        hi16[j] = (key >> 16).astype(I16)

    _for_each_tile(nt, score_tile)

    thr = _topk_threshold(keys, hi16, lo16, nt, tq, tk, topk)

    m_s[...] = jnp.full(m_s.shape, NEG_BIG, F32)
    l_s[...] = jnp.zeros(l_s.shape, F32)
    acc[...] = jnp.zeros(acc.shape, F32)
    q_st = [jnp.concatenate([qb_ref[:, (kv * G_B + g) * HD_B:(kv * G_B + g + 1) * HD_B] for g in range(G_B)], axis=0)
            for kv in range(KV_B)]

    def attn_tile(j):
        start = pl.multiple_of(j * tk, tk)
        kt = keys[j]
        sel = kt >= thr
        sel = jnp.concatenate([sel] * G_B, axis=0)
        for kv in range(KV_B):
            s = _dot_t(q_st[kv], kb_ref[pl.ds(start, tk), kv * HD_B:(kv + 1) * HD_B])
            vt = vb_ref[pl.ds(start, tk), kv * HD_B:(kv + 1) * HD_B]
            if online:
                _online_softmax_step(jnp.where(sel, s, -jnp.inf), vt, m_s, l_s, acc, kv)
            else:
                _bounded_softmax_step(s, sel, vt, l_s, acc, kv)

    _for_each_tile(nt, attn_tile)
    for kv in range(KV_B):
        o = acc[kv] / _row_sum(l_s[kv], online)
        for g in range(G_B):
            hg = kv * G_B + g
            o_ref[:, hg * HD_B:(hg + 1) * HD_B] = o[g * tq:(g + 1) * tq]


def _prompt_dsa(qi, wi, qb, ki, kb, vb, tq, tk, online):
    t = qb.shape[0]
    topk = min(TOPK_MAX, t // 4)
    kern = functools.partial(_pdsa_kernel, tq=tq, tk=tk, topk=topk, online=online)
    row = lambda w: pl.BlockSpec((tq, w), lambda i: (i, 0))
    return pl.pallas_call(
        kern,
        out_shape=jax.ShapeDtypeStruct((t, H_B * HD_B), F32),
        grid=(t // tq,),
        in_specs=[row(W_QI), row(LANES), row(W_QB), _const_spec(ki.shape), _const_spec(kb.shape),
                  _const_spec(vb.shape)],
        out_specs=row(H_B * HD_B),
        scratch_shapes=[pltpu.VMEM((t // tk, tq, tk), I32), pltpu.VMEM((t // tk, tq, tk), I16),
                        pltpu.VMEM((t // tk, tq, tk), I16), pltpu.VMEM((KV_B, G_B * tq, LANES), F32),
                        pltpu.VMEM((KV_B, G_B * tq, LANES), F32), pltpu.VMEM((KV_B, G_B * tq, LANES), F32)],
        compiler_params=_cparams(("parallel",)),
        name="prompt_dsa",
    )(qi, wi, qb, ki, kb, vb)


def _fetch_pages(pt_ref, pools, bufs, sems, *, pps, nsteps, total):
    t = pl.program_id(0) * nsteps + pl.program_id(1)
    slot = t % 2

    def copies(tt, s, p):
        pg = pt_ref[tt // nsteps, (tt % nsteps) * pps + p]
        return [pltpu.make_async_copy(pool.at[pg], buf.at[s, p], sem.at[s])
                for pool, buf, sem in zip(pools, bufs, sems)]

    def start(tt, s):
        def body(p, carry):
            for cp in copies(tt, s, p):
                cp.start()
            return carry
        lax.fori_loop(0, pps, body, 0)

    @pl.when(t == 0)
    def _():
        start(t, slot)

    @pl.when(t + 1 < total)
    def _():
        start(t + 1, 1 - slot)

    def wait(p, carry):
        for cp in copies(t, slot, p):
            cp.wait()
        return carry

    lax.fori_loop(0, pps, wait, 0)
    return slot


def _page_scratch(pools, pps):
    bufs = [pltpu.VMEM((2, pps) + pool.shape[1:], pool.dtype) for pool in pools]
    sems = [pltpu.SemaphoreType.DMA((2,)) for _ in pools]
    return bufs + sems


def _sdiff_kernel(pt_ref, lq1, lk1, lq2, lk2, q_ref, kn_ref, vn_ref, kpool, vpool, o_ref,
                  kbuf, vbuf, ksem, vsem, kc, vc, lhs, m_s, l_s, acc, *, pps, page, tn, lam_init, nsteps, total):
    j = pl.program_id(1)

    @pl.when(j == 0)
    def _():
        q = q_ref[0].astype(F32)
        for kv in range(KV_A):
            parts = [_split_components(q[:, (kv * G_A + g) * LANES:(kv * G_A + g + 1) * LANES]) for g in range(G_A)]
            lhs[kv] = jnp.concatenate(parts, axis=0).astype(BF16)
        m_s[...] = jnp.full(m_s.shape, NEG_BIG, F32)
        l_s[...] = jnp.zeros(l_s.shape, F32)
        acc[...] = jnp.zeros(acc.shape, F32)

    slot = _fetch_pages(pt_ref, [kpool, vpool], [kbuf, vbuf], [ksem, vsem], pps=pps, nsteps=nsteps, total=total)
    for p in range(pps):
        for kv in range(KV_A):
            kc[kv, :, p * page:(p + 1) * page] = kbuf[slot, p, kv].reshape(2 * HD_A, page).astype(BF16)
            vc[kv, p * page:(p + 1) * page, :] = vbuf[slot, p, pl.ds(kv, page, stride=KV_A), :].astype(BF16)
    for kv in range(KV_A):
        s = _dot(lhs[kv], kc[kv])
        _online_softmax_step(s, vc[kv], m_s, l_s, acc, kv)

    @pl.when(j == pl.num_programs(1) - 1)
    def _():
        kn = kn_ref[0]
        vn = vn_ref[0]
        lam = _lambda(lq1, lk1, lq2, lk2, lam_init)
        for kv in range(KV_A):
            s = _dot_t(lhs[kv], kn[:, kv * LANES:(kv + 1) * LANES])
            r = lax.broadcasted_iota(I32, s.shape, 0) % tn
            c = lax.broadcasted_iota(I32, s.shape, 1)
            s = jnp.where(c <= r, s, -jnp.inf)
            _online_softmax_step(s, vn[:, kv * LANES:(kv + 1) * LANES], m_s, l_s, acc, kv)
            o = acc[kv] / l_s[kv]
            for g in range(G_A):
                hg = kv * G_A + g
                o_ref[0, :, hg * LANES:(hg + 1) * LANES] = (o[(2 * g) * tn:(2 * g + 1) * tn]
                                                            - lam * o[(2 * g + 1) * tn:(2 * g + 2) * tn])


def _sample_diff_attn(page_table, lams, qa, ka_new, va_new, pool_kt, pool_v, lam_init, pps):
    nb, tn, _ = qa.shape
    n_pages = page_table.shape[1]
    page = pool_kt.shape[-1]
    nsteps = n_pages // pps
    kern = functools.partial(_sdiff_kernel, pps=pps, page=page, tn=tn, lam_init=lam_init, nsteps=nsteps,
                             total=nb * nsteps)
    seq = lambda w: pl.BlockSpec((1, tn, w), lambda b, j, pt: (b, 0, 0))
    cst = lambda shape: pl.BlockSpec(shape, lambda b, j, pt: (0,) * len(shape))
    hbm = pl.BlockSpec(memory_space=pl.ANY)
    rows = 2 * G_A * tn
    return pl.pallas_call(
        kern,
        out_shape=jax.ShapeDtypeStruct((nb, tn, H_A * DV_A), F32),
        grid_spec=pltpu.PrefetchScalarGridSpec(
            num_scalar_prefetch=1,
            grid=(nb, nsteps),
            in_specs=[cst((1, HD_A))] * 4 + [seq(W_QA), seq(W_KA), seq(W_VA), hbm, hbm],
            out_specs=seq(H_A * DV_A),
            scratch_shapes=_page_scratch([pool_kt, pool_v], pps)
                           + [pltpu.VMEM((KV_A, 2 * HD_A, pps * page), BF16),
                              pltpu.VMEM((KV_A, pps * page, DV_A), BF16),
                              pltpu.VMEM((KV_A, rows, LANES), BF16), pltpu.VMEM((KV_A, rows, LANES), F32),
                              pltpu.VMEM((KV_A, rows, LANES), F32), pltpu.VMEM((KV_A, rows, LANES), F32)]),
        compiler_params=_cparams(("arbitrary", "arbitrary")),
        name="sample_diff_attn",
    )(page_table, *lams, qa, ka_new, va_new, pool_kt, pool_v)


def _sscore_kernel(pt_ref, qi_ref, wi_ref, kin_ref, ipool, keys_o, newkeys_o, ibuf, isem,
                   *, pps, page, tn, tile, nsteps, total):
    j = pl.program_id(1)
    q = qi_ref[0].astype(F32)
    qi_st = jnp.concatenate([q[:, h * D_IDX:(h + 1) * D_IDX] for h in range(H_IDX)], axis=0).astype(BF16)
    wv = wi_ref[0]
    w_cols = [wv[:, D_IDX + h:D_IDX + h + 1] for h in range(H_IDX)]

    def scores(lg):
        sc = jnp.zeros((tn, lg.shape[1]), F32)
        for h in range(H_IDX):
            sc = sc + jnp.maximum(lg[h * tn:(h + 1) * tn], 0.0) * w_cols[h]
        return _sortable_key(sc)

    slot = _fetch_pages(pt_ref, [ipool], [ibuf], [isem], pps=pps, nsteps=nsteps, total=total)
    kt = jnp.concatenate([ibuf[slot, p] for p in range(pps)], axis=1).astype(BF16)
    keys = scores(_dot(qi_st, kt))
    for c in range(pps * page // tile):
        keys_o[c] = keys[:, c * tile:(c + 1) * tile]

    @pl.when(j == nsteps - 1)
    def _():
        nk = scores(_dot_t(qi_st, kin_ref[0]))
        r = lax.broadcasted_iota(I32, nk.shape, 0)
        c = lax.broadcasted_iota(I32, nk.shape, 1)
        nk = jnp.where(c <= r, nk, INT_MIN)
        newkeys_o[...] = jnp.concatenate([nk, jnp.full((tn, LANES - tn), INT_MIN, I32)], axis=1)


def _sample_scores(page_table, qi, wi, ki_new, pool_it, pps, tile):
    nb, tn, _ = qi.shape
    n_pages = page_table.shape[1]
    page = pool_it.shape[2]
    nsteps = n_pages // pps
    tiles_per_step = pps * page // tile
    kern = functools.partial(_sscore_kernel, pps=pps, page=page, tn=tn, tile=tile, nsteps=nsteps,
                             total=nb * nsteps)
    seq = lambda w: pl.BlockSpec((1, tn, w), lambda b, j, pt: (b, 0, 0))
    return pl.pallas_call(
        kern,
        out_shape=[jax.ShapeDtypeStruct((n_pages * page // tile, nb * tn, tile), I32),
                   jax.ShapeDtypeStruct((nb * tn, LANES), I32)],
        grid_spec=pltpu.PrefetchScalarGridSpec(
            num_scalar_prefetch=1,
            grid=(nb, nsteps),
            in_specs=[seq(W_QI), seq(LANES), seq(D_IDX), pl.BlockSpec(memory_space=pl.ANY)],
            out_specs=[pl.BlockSpec((tiles_per_step, tn, tile), lambda b, j, pt: (j, b, 0)),
                       pl.BlockSpec((tn, LANES), lambda b, j, pt: (b, 0))],
            scratch_shapes=_page_scratch([pool_it], pps)),
        compiler_params=_cparams(("arbitrary", "arbitrary")),
        name="sample_scores",
    )(page_table, qi, wi, ki_new, pool_it)


def _sthr_kernel(keys_ref, newkeys_ref, sel_ref, keys, hi16, lo16, *, rows, tile, nt, topk):
    for j in range(nt - 1):
        keys[j] = keys_ref[j]
    pad = [jnp.full((rows, tile - LANES), INT_MIN, I32)] if tile > LANES else []
    keys[nt - 1] = jnp.concatenate([newkeys_ref[...]] + pad, axis=1)

    def split(j, carry):
        hi16[j] = (keys[j] >> 16).astype(I16)
        return carry

    lax.fori_loop(0, nt, split, 0)
    thr = _topk_threshold(keys, hi16, lo16, nt, rows, tile, topk)
    for j in range(nt - 1):
        sel_ref[:, j * tile:(j + 1) * tile] = (keys[j] >= thr).astype(F32)
    sel_ref[:, (nt - 1) * tile:] = (keys[nt - 1][:, :LANES] >= thr).astype(F32)


def _sample_select(keys, newkeys, topk):
    n_tiles, rows_all, tile = keys.shape
    rows = _tile(rows_all, 128)
    nt = n_tiles + 1
    kern = functools.partial(_sthr_kernel, rows=rows, tile=tile, nt=nt, topk=topk)
    width = n_tiles * tile + LANES
    return pl.pallas_call(
        kern,
        out_shape=jax.ShapeDtypeStruct((rows_all, width), F32),
        grid=(rows_all // rows,),
        in_specs=[pl.BlockSpec((n_tiles, rows, tile), lambda i: (0, i, 0)),
                  pl.BlockSpec((rows, LANES), lambda i: (i, 0))],
        out_specs=pl.BlockSpec((rows, width), lambda i: (i, 0)),
        scratch_shapes=[pltpu.VMEM((nt, rows, tile), I32), pltpu.VMEM((nt, rows, tile), I16),
                        pltpu.VMEM((nt, rows, tile), I16)],
        compiler_params=_cparams(("parallel",)),
        name="sample_select",
    )(keys, newkeys)


def _sdsa_kernel(pt_ref, q_ref, sel_ref, seln_ref, kn_ref, vn_ref, kpool, vpool, o_ref,
                 kbuf, vbuf, ksem, vsem, kc, vc, lhs, m_s, l_s, acc, *, pps, page, tn, nsteps, total):
    j = pl.program_id(1)

    @pl.when(j == 0)
    def _():
        q = q_ref[0].astype(F32)
        for kv in range(KV_B):
            parts = [q[:, (kv * G_B + g) * HD_B:(kv * G_B + g + 1) * HD_B] for g in range(G_B)]
            lhs[kv] = jnp.concatenate(parts, axis=0).astype(BF16)
        m_s[...] = jnp.full(m_s.shape, NEG_BIG, F32)
        l_s[...] = jnp.zeros(l_s.shape, F32)
        acc[...] = jnp.zeros(acc.shape, F32)

    def step(sel, kmats, vmats):
        sel = jnp.concatenate([sel > 0.5] * G_B, axis=0)
        for kv in range(KV_B):
            s = jnp.where(sel, _dot_t(lhs[kv], kmats[kv]), -jnp.inf)
            _online_softmax_step(s, vmats[kv], m_s, l_s, acc, kv)

    slot = _fetch_pages(pt_ref, [kpool, vpool], [kbuf, vbuf], [ksem, vsem], pps=pps, nsteps=nsteps, total=total)
    for p in range(pps):
        for kv in range(KV_B):
            kc[kv, p * page:(p + 1) * page, :] = kbuf[slot, p, pl.ds(kv, page, stride=KV_B), :].astype(BF16)
            vc[kv, p * page:(p + 1) * page, :] = vbuf[slot, p, pl.ds(kv, page, stride=KV_B), :].astype(BF16)
    step(sel_ref[0], [kc[kv] for kv in range(KV_B)], [vc[kv] for kv in range(KV_B)])

    @pl.when(j == nsteps - 1)
    def _():
        kn, vn = kn_ref[0], vn_ref[0]
        step(seln_ref[0][:, :tn], [kn[:, kv * HD_B:(kv + 1) * HD_B] for kv in range(KV_B)],
             [vn[:, kv * HD_B:(kv + 1) * HD_B] for kv in range(KV_B)])
        for kv in range(KV_B):
            o = acc[kv] / l_s[kv]
            for g in range(G_B):
                hg = kv * G_B + g
                o_ref[0, :, hg * HD_B:(hg + 1) * HD_B] = o[g * tn:(g + 1) * tn]


def _sample_dsa(page_table, qb, sel, kb_new, vb_new, pool_k, pool_v, pps):
    nb, tn, _ = qb.shape
    n_pages = page_table.shape[1]
    page = pool_k.shape[1] // KV_B
    tk = pps * page
    nsteps = n_pages // pps
    kern = functools.partial(_sdsa_kernel, pps=pps, page=page, tn=tn, nsteps=nsteps, total=nb * nsteps)
    seq = lambda w: pl.BlockSpec((1, tn, w), lambda b, j, pt: (b, 0, 0))
    hbm = pl.BlockSpec(memory_space=pl.ANY)
    rows = G_B * tn
    return pl.pallas_call(
        kern,
        out_shape=jax.ShapeDtypeStruct((nb, tn, H_B * HD_B), F32),
        grid_spec=pltpu.PrefetchScalarGridSpec(
            num_scalar_prefetch=1,
            grid=(nb, nsteps),
            in_specs=[seq(W_QB), pl.BlockSpec((1, tn, tk), lambda b, j, pt: (b, 0, j)),
                      pl.BlockSpec((1, tn, LANES), lambda b, j, pt: (b, 0, (n_pages * page) // LANES)),
                      seq(W_KB), seq(W_KB), hbm, hbm],
            out_specs=seq(H_B * HD_B),
            scratch_shapes=_page_scratch([pool_k, pool_v], pps)
                           + [pltpu.VMEM((KV_B, tk, HD_B), BF16), pltpu.VMEM((KV_B, tk, HD_B), BF16),
                              pltpu.VMEM((KV_B, rows, LANES), BF16), pltpu.VMEM((KV_B, rows, LANES), F32),
                              pltpu.VMEM((KV_B, rows, LANES), F32), pltpu.VMEM((KV_B, rows, LANES), F32)]),
        compiler_params=_cparams(("arbitrary", "arbitrary")),
        name="sample_dsa",
    )(page_table, qb, sel, sel, kb_new, vb_new, pool_k, pool_v)


def _finish_kernel(x_ref, oa_ref, ob_ref, ga_ref, gb_ref, sub_ref, gm_ref, wa, wb, wo, wu, wd, y_ref, *, lam_init):
    oa = oa_ref[...]
    sub = sub_ref[...]
    chunks = []
    for h in range(H_A):
        c = oa[:, h * DV_A:(h + 1) * DV_A]
        ms = jnp.mean(c * c, axis=-1, keepdims=True)
        chunks.append(((c * lax.rsqrt(ms + EPS)) * sub) * (1.0 - lam_init))
    oan = jnp.concatenate(chunks, axis=1).astype(BF16)
    pa = _dot(oan, wa[...])
    pb = _dot(ob_ref[...].astype(BF16), wb[...])
    mix = (ga_ref[...] * pa + gb_ref[...] * pb).astype(BF16)
    x1 = x_ref[...] + _dot(mix, wo[...])
    ms = jnp.mean(x1 * x1, axis=-1, keepdims=True)
    xn = ((x1 * lax.rsqrt(ms + EPS)) * gm_ref[...]).astype(BF16)
    hid = jnp.maximum(_dot(xn, wu[...]), 0.0)
    y_ref[...] = x1 + _dot((hid * hid).astype(BF16), wd[...])


def _finish(x, oa, ob, ga, gb, pw, lam_init, tm):
    t, d = x.shape
    row = pl.BlockSpec((tm, d), lambda i: (i, 0))
    weights = [pw[k] for k in ("w_a_proj", "w_b_proj", "w_out", "w_up", "w_down")]
    kern = functools.partial(_finish_kernel, lam_init=lam_init)
    return pl.pallas_call(
        kern,
        out_shape=jax.ShapeDtypeStruct((t, d), F32),
        grid=(t // tm,),
        in_specs=[row] * 5 + [_const_spec((1, DV_A)), _const_spec((1, d))] + [_const_spec(w.shape) for w in weights],
        out_specs=row,
        compiler_params=_cparams(("parallel",)),
        name="finish",
    )(x, oa, ob, ga, gb, pw["sub_norm_a"], pw["g_mlp"], *weights)


def _prep_weights(l, g_attn_norm, w_in, q_norm_a, k_norm_a, sub_norm_a, q_norm_b, k_norm_b, idx_k_norm,
                  w_a_proj, w_b_proj, w_out, g_mlp_norm, w_up, w_down):
    w = w_in[l].astype(BF16)
    widths = (W_QA, W_KA, W_VA, W_QB, W_KB, W_KB, W_QI, D_IDX, H_IDX, w.shape[0], w.shape[0])
    offs = np.concatenate([[0], np.cumsum(widths)])
    seg = [w[:, int(offs[i]):int(offs[i + 1])] for i in range(len(widths))]
    wkw = jnp.concatenate([seg[7], seg[8], jnp.zeros((w.shape[0], LANES - D_IDX - H_IDX), BF16)], axis=1)
    lane = np.arange(LANES)
    row2 = lambda v: v.astype(F32)[None, :]
    return {
        "g_attn": row2(g_attn_norm[l]),
        "wqa": seg[0], "wka": seg[1], "wva": seg[2], "wqb": seg[3], "wkb": seg[4], "wvb": seg[5],
        "wqi": seg[6], "wkw": wkw, "wga": seg[9], "wgb": seg[10],
        "gqa": row2(jnp.tile(q_norm_a[l], W_QA // HD_A) * (HD_A ** -0.5 * LOG2E)),
        "gka": row2(jnp.tile(k_norm_a[l], W_KA // HD_A)),
        "gqb": row2(jnp.tile(q_norm_b[l], W_QB // HD_B) * (HD_B ** -0.5 * LOG2E)),
        "gkb": row2(jnp.tile(k_norm_b[l], W_KB // HD_B)),
        "gki": row2(jnp.concatenate([idx_k_norm[l], jnp.zeros((LANES - D_IDX,), F32)])),
        "ones64": jnp.asarray((lane[:, None] // HD_A) == (lane[None, :] // HD_A), BF16),
        "ones128": jnp.ones((LANES, LANES), BF16),
        "sub_norm_a": row2(sub_norm_a[l]), "g_mlp": row2(g_mlp_norm[l]),
        "w_a_proj": w_a_proj[l].astype(BF16), "w_b_proj": w_b_proj[l].astype(BF16),
        "w_out": w_out[l].astype(BF16), "w_up": w_up[l].astype(BF16), "w_down": w_down[l].astype(BF16),
    }


def _tile(n, pref):
    return pref if n % pref == 0 else n


def _tiles(seq, n_sample_rows, n_pages):
    pps = next(p for p in (32, 16, 8, 4, 2, 1) if n_pages % p == 0)
    return {"proj": _tile(seq, 256), "diff": _tile(seq, 256), "diff_k": _tile(seq, 512),
            "dsa_q": _tile(seq, 128), "dsa_k": _tile(seq, 512),
            "finish": _tile(seq, 256), "sample_rows": _tile(n_sample_rows, 256), "pps": pps,
            "sel_tile": _tile(pps * 128, 512)}


def _score_bound(q_gain, k_gain, head_dim):
    return (head_dim ** 0.5) * jnp.max(jnp.abs(q_gain)) * jnp.max(jnp.abs(k_gain))


def kernel(x_prompt, x_sample, cache_a_k, cache_a_v, cache_b_k, cache_b_v, cache_b_idx_k, page_table, g_attn_norm, w_in, q_norm_a, k_norm_a, lam_q1, lam_k1, lam_q2, lam_k2, sub_norm_a, q_norm_b, k_norm_b, idx_k_norm, w_a_proj, w_b_proj, w_out, g_mlp_norm, w_up, w_down):
    depth = w_in.shape[0]
    nb_p, seq, d = x_prompt.shape
    nb_s, tn, _ = x_sample.shape
    n_pages = page_table.shape[1]
    n_phys, page = cache_a_k.shape[1], cache_a_k.shape[2]
    past = n_pages * page
    assert nb_p == 1
    pos_p = jnp.arange(seq)
    pos_s = past + (jnp.arange(nb_s * tn) % tn)
    tl = _tiles(seq, nb_s * tn, n_pages)
    pps = tl["pps"]

    xp = x_prompt.reshape(seq, d)
    xs = x_sample.reshape(nb_s * tn, d)
    outs_p = [[] for _ in range(5)]
    outs_s = [[] for _ in range(5)]
    for l in range(depth):
        lam_init = 0.8 - 0.6 * math.exp(-0.3 * l)
        pw = _prep_weights(l, g_attn_norm, w_in, q_norm_a, k_norm_a, sub_norm_a, q_norm_b, k_norm_b, idx_k_norm,
                           w_a_proj, w_b_proj, w_out, g_mlp_norm, w_up, w_down)
        lams = [v[l].astype(F32)[None, :] for v in (lam_q1, lam_k1, lam_q2, lam_k2)]

        (qa, kaf, kab, vaf, vab, qb, kbf, kbb, vbf, vbb, qi, kif, kib, wi, ga, gb) = _project(
            xs, pos_s, pw, tl["sample_rows"])
        seq3 = lambda a: a.reshape(nb_s, tn, a.shape[-1])
        pool_akt = jnp.transpose(cache_a_k[l], (0, 2, 3, 4, 1))
        pool_it = jnp.transpose(cache_b_idx_k[l], (0, 2, 1))
        rows2 = lambda c: c[l].reshape(n_phys, page * c.shape[3], c.shape[4])
        oa = _sample_diff_attn(page_table, lams, seq3(qa), seq3(kab), seq3(vab), pool_akt, rows2(cache_a_v),
                               lam_init, pps)
        keys, newkeys = _sample_scores(page_table, seq3(qi), seq3(wi), seq3(kib), pool_it, pps, tl["sel_tile"])
        sel = seq3(_sample_select(keys, newkeys, min(TOPK_MAX, (past + tn) // 4)))
        ob = _sample_dsa(page_table, seq3(qb), sel, seq3(kbb), seq3(vbb), rows2(cache_b_k), rows2(cache_b_v), pps)
        xs = _finish(xs, oa.reshape(nb_s * tn, -1), ob.reshape(nb_s * tn, -1), ga, gb, pw, lam_init,
                     tl["sample_rows"])
        for lst, v in zip(outs_s, (kaf.reshape(nb_s, tn, KV_A, 2, HD_A), vaf.reshape(nb_s, tn, KV_A, DV_A),
                                   kbf.reshape(nb_s, tn, KV_B, HD_B), vbf.reshape(nb_s, tn, KV_B, HD_B),
                                   kif.reshape(nb_s, tn, D_IDX))):
            lst.append(v)

        (qa, kaf, kab, vaf, vab, qb, kbf, kbb, vbf, vbb, qi, kif, kib, wi, ga, gb) = _project(
            xp, pos_p, pw, tl["proj"])
        diff = lambda online: functools.partial(_prompt_diff_attn, lams, qa, kab, vab, lam_init, tl["diff"], tl["diff_k"],
                                                online)
        oa = lax.cond(_score_bound(q_norm_a[l], k_norm_a[l], HD_A) <= MAX_BOUNDED_SCORE, diff(False), diff(True))
        dsa = lambda online: functools.partial(_prompt_dsa, qi, wi, qb, kib, kbb, vbb, tl["dsa_q"], tl["dsa_k"],
                                               online)
        ob = lax.cond(_score_bound(q_norm_b[l], k_norm_b[l], HD_B) <= MAX_BOUNDED_SCORE, dsa(False), dsa(True))
        xp = _finish(xp, oa, ob, ga, gb, pw, lam_init, tl["finish"])
        for lst, v in zip(outs_p, (kaf.reshape(1, seq, KV_A, 2, HD_A), vaf.reshape(1, seq, KV_A, DV_A),
                                   kbf.reshape(1, seq, KV_B, HD_B), vbf.reshape(1, seq, KV_B, HD_B),
                                   kif.reshape(1, seq, D_IDX))):
            lst.append(v)

    return (xp.reshape(nb_p, seq, d), xs.reshape(nb_s, tn, d),
            *[jnp.stack(v) for v in outs_p], *[jnp.stack(v) for v in outs_s])
```

```python
import functools
import math

import jax
import jax.numpy as jnp
import numpy as np
from jax import lax
from jax.experimental import pallas as pl
from jax.experimental.pallas import tpu as pltpu

F32 = jnp.float32
BF16 = jnp.bfloat16
I32 = jnp.int32
I16 = jnp.int16

LANES = 128
VMEM_LIMIT = 56 * 1024 * 1024
EPS = 1e-6
ROPE_THETA = 10000.0
TOPK_MAX = 256
NEG_BIG = -1e30
INT_MIN = -(2 ** 31)
LOG2E = 1.4426950408889634
MAX_BOUNDED_SCORE = 60.0

H_A, KV_A, HD_A = 8, 2, 64
G_A = H_A // KV_A
DV_A = 2 * HD_A
H_B, KV_B, HD_B = 8, 2, 128
G_B = H_B // KV_B
H_IDX, D_IDX = 8, 64
W_QA = H_A * 2 * HD_A
W_KA = KV_A * 2 * HD_A
W_VA = KV_A * DV_A
W_QB = H_B * HD_B
W_KB = KV_B * HD_B
W_QI = H_IDX * D_IDX


def _cparams(sem):
    return pltpu.CompilerParams(dimension_semantics=sem, vmem_limit_bytes=VMEM_LIMIT)


def _const_spec(shape):
    nd = len(shape)
    return pl.BlockSpec(shape, lambda *_: (0,) * nd, pipeline_mode=pl.Buffered(1))


def _dot_t(a, b):
    return lax.dot_general(a, b, (((1,), (1,)), ((), ())), preferred_element_type=F32)


def _dot(a, b):
    return jnp.dot(a, b, preferred_element_type=F32)


def _group_rms(h, ones_ref, inv_group):
    outs = []
    for c in range(h.shape[1] // LANES):
        hs = h[:, c * LANES:(c + 1) * LANES]
        sq = hs * hs
        hi = sq.astype(BF16)
        lo = (sq - hi.astype(F32)).astype(BF16)
        ss = _dot(hi, ones_ref[...]) + _dot(lo, ones_ref[...])
        outs.append(hs * lax.rsqrt(ss * inv_group + EPS))
    return outs


def _rope_chunks(chunks, gain, cos, sin, half):
    lane = lax.broadcasted_iota(I32, chunks[0].shape, 1)
    first = (lane % (2 * half)) < half
    outs = []
    for c, y in enumerate(chunks):
        y = y * gain[:, c * LANES:(c + 1) * LANES]
        if half == LANES // 2:
            partner = pltpu.roll(y, half, 1)
        else:
            partner = jnp.where(first, pltpu.roll(y, LANES - half, 1), pltpu.roll(y, half, 1))
        outs.append(y * cos + partner * sin)
    return outs


def _proj_kernel(x_ref, g_ref, wqa, wka, wva, wqb, wkb, wvb, wqi, wkw, wga, wgb,
                 gqa, gka, gqb, gkb, gki, cos_a, sin_a, cos_b, sin_b, ones64, ones128,
                 qa_o, kaf_o, kab_o, vaf_o, vab_o, qb_o, kbf_o, kbb_o, vbf_o, vbb_o,
                 qi_o, kif_o, kib_o, wi_o, ga_o, gb_o):
    x = x_ref[...]
    ms = jnp.mean(x * x, axis=-1, keepdims=True)
    xn = ((x * lax.rsqrt(ms + EPS)) * g_ref[...]).astype(BF16)
    ca, sa, cb, sb = cos_a[...], sin_a[...], cos_b[...], sin_b[...]

    def cat(chunks):
        return chunks[0] if len(chunks) == 1 else jnp.concatenate(chunks, axis=1)

    h = _dot(xn, wqa[...])
    qa_o[...] = cat(_rope_chunks(_group_rms(h, ones64, 1.0 / HD_A), gqa[...], ca, sa, HD_A // 2)).astype(BF16)
    h = _dot(xn, wka[...])
    ka = cat(_rope_chunks(_group_rms(h, ones64, 1.0 / HD_A), gka[...], ca, sa, HD_A // 2))
    kaf_o[...] = ka
    kab_o[...] = ka.astype(BF16)
    h = _dot(xn, wva[...])
    vaf_o[...] = h
    vab_o[...] = h.astype(BF16)
    h = _dot(xn, wqb[...])
    qb_o[...] = cat(_rope_chunks(_group_rms(h, ones128, 1.0 / HD_B), gqb[...], cb, sb, HD_B // 2)).astype(BF16)
    h = _dot(xn, wkb[...])
    kb = cat(_rope_chunks(_group_rms(h, ones128, 1.0 / HD_B), gkb[...], cb, sb, HD_B // 2))
    kbf_o[...] = kb
    kbb_o[...] = kb.astype(BF16)
    h = _dot(xn, wvb[...])
    vbf_o[...] = h
    vbb_o[...] = h.astype(BF16)
    h = _dot(xn, wqi[...])
    chunks = [h[:, c * LANES:(c + 1) * LANES] for c in range(W_QI // LANES)]
    ones_gain = jnp.full((1, W_QI), D_IDX ** -0.5, F32)
    qi_o[...] = cat(_rope_chunks(chunks, ones_gain, ca, sa, D_IDX // 2)).astype(BF16)
    h = _dot(xn, wkw[...])
    ki = _rope_chunks(_group_rms(h, ones64, 1.0 / D_IDX), gki[...], ca, sa, D_IDX // 2)[0]
    kif_o[...] = ki[:, :D_IDX]
    kib_o[...] = ki[:, :D_IDX].astype(BF16)
    wi_o[...] = h * (H_IDX ** -0.5)
    ga_o[...] = jax.nn.sigmoid(_dot(xn, wga[...]))
    gb_o[...] = jax.nn.sigmoid(_dot(xn, wgb[...]))


def _rope_tables(pos, half, group):
    inv_freq = ROPE_THETA ** (-jnp.arange(half, dtype=F32) / half)
    ang = pos.astype(F32)[:, None] * inv_freq[None, :]
    cos, sin = jnp.cos(ang), jnp.sin(ang)
    reps = LANES // group
    cos_t = jnp.tile(jnp.concatenate([cos, cos], axis=1), (1, reps))
    sin_t = jnp.tile(jnp.concatenate([-sin, sin], axis=1), (1, reps))
    return cos_t, sin_t


def _project(x, pos, pw, tm):
    t, d = x.shape
    cos_a, sin_a = _rope_tables(pos, HD_A // 2, HD_A)
    cos_b, sin_b = _rope_tables(pos, HD_B // 2, HD_B)
    row = lambda w: pl.BlockSpec((tm, w), lambda i: (i, 0))
    weights = [pw[k] for k in ("wqa", "wka", "wva", "wqb", "wkb", "wvb", "wqi", "wkw", "wga", "wgb")]
    consts = [pw[k] for k in ("gqa", "gka", "gqb", "gkb", "gki")]
    in_specs = ([row(d), _const_spec((1, d))] + [_const_spec(w.shape) for w in weights]
                + [_const_spec(c.shape) for c in consts] + [row(LANES)] * 4
                + [_const_spec((LANES, LANES))] * 2)
    outs = [(W_QA, BF16), (W_KA, F32), (W_KA, BF16), (W_VA, F32), (W_VA, BF16),
            (W_QB, BF16), (W_KB, F32), (W_KB, BF16), (W_KB, F32), (W_KB, BF16),
            (W_QI, BF16), (D_IDX, F32), (D_IDX, BF16), (LANES, F32), (d, F32), (d, F32)]
    return pl.pallas_call(
        _proj_kernel,
        out_shape=[jax.ShapeDtypeStruct((t, w), dt) for w, dt in outs],
        grid=(t // tm,),
        in_specs=in_specs,
        out_specs=[row(w) for w, _ in outs],
        compiler_params=_cparams(("parallel",)),
        name="proj",
    )(x, pw["g_attn"], *weights, *consts, cos_a, sin_a, cos_b, sin_b, pw["ones64"], pw["ones128"])


def _lane_rep(col):
    return jnp.broadcast_to(col, (col.shape[0], LANES))


def _online_softmax_step(s, v, m_ref, l_ref, acc_ref, idx):
    m_old = m_ref[idx]
    m_new = jnp.maximum(m_old, _lane_rep(jnp.max(s, axis=1, keepdims=True)))
    alpha = jnp.exp2(m_old - m_new)
    p = jnp.exp2(s - m_new[:, :1])
    l_ref[idx] = alpha * l_ref[idx] + _lane_rep(jnp.sum(p, axis=1, keepdims=True))
    acc_ref[idx] = alpha * acc_ref[idx] + _dot(p.astype(BF16), v)
    m_ref[idx] = m_new


def _bounded_softmax_step(s, keep, v, l_ref, acc_ref, idx):
    p = jnp.exp2(s)
    if keep is not None:
        p = jnp.where(keep, p, 0.0)
    part = p[:, :LANES]
    for c in range(1, p.shape[1] // LANES):
        part = part + p[:, c * LANES:(c + 1) * LANES]
    l_ref[idx] = l_ref[idx] + part
    acc_ref[idx] = acc_ref[idx] + _dot(p.astype(BF16), v)


def _row_sum(l, online):
    return l[:, :1] if online else jnp.sum(l, axis=1, keepdims=True)


TILES_PER_TRIP = 4


def _for_each_tile(nt, fn):
    def group(jj, carry):
        for u in range(TILES_PER_TRIP):
            fn(TILES_PER_TRIP * jj + u)
        return carry

    def single(j, carry):
        fn(j)
        return carry

    lax.fori_loop(0, nt // TILES_PER_TRIP, group, 0)
    lax.fori_loop(nt - nt % TILES_PER_TRIP, nt, single, 0)


def _split_components(qg):
    lane = lax.broadcasted_iota(I32, qg.shape, 1)
    zero = jnp.zeros_like(qg)
    return jnp.concatenate([jnp.where(lane < HD_A, qg, zero), jnp.where(lane >= HD_A, qg, zero)], axis=0)


def _lambda(lq1, lk1, lq2, lk2, lam_init):
    a = jnp.exp(jnp.sum(lq1[...] * lk1[...], axis=1, keepdims=True))
    b = jnp.exp(jnp.sum(lq2[...] * lk2[...], axis=1, keepdims=True))
    return a - b + lam_init


def _pdiff_kernel(lq1, lk1, lq2, lk2, q_ref, k_ref, v_ref, o_ref, lhs, m_s, l_s, acc,
                  *, tq, tk, lam_init, online):
    i = pl.program_id(0)
    for hg in range(H_A):
        lhs[hg] = _split_components(q_ref[:, hg * LANES:(hg + 1) * LANES])
    m_s[...] = jnp.full(m_s.shape, NEG_BIG, F32)
    l_s[...] = jnp.zeros(l_s.shape, F32)
    acc[...] = jnp.zeros(acc.shape, F32)

    def tile(j, masked):
        start = pl.multiple_of(j * tk, tk)
        for kv in range(KV_A):
            kt = k_ref[pl.ds(start, tk), kv * LANES:(kv + 1) * LANES]
            vt = v_ref[pl.ds(start, tk), kv * LANES:(kv + 1) * LANES]
            for g in range(G_A):
                hg = kv * G_A + g
                s = _dot_t(lhs[hg], kt)
                keep = None
                if masked:
                    r = i * tq + lax.broadcasted_iota(I32, s.shape, 0) % tq
                    c = j * tk + lax.broadcasted_iota(I32, s.shape, 1)
                    keep = c <= r
                if online:
                    if masked:
                        s = jnp.where(keep, s, -jnp.inf)
                    _online_softmax_step(s, vt, m_s, l_s, acc, hg)
                else:
                    _bounded_softmax_step(s, keep, vt, l_s, acc, hg)

    def body(j, carry):
        tile(j, False)
        return carry

    n_full = (i * tq) // tk
    lax.fori_loop(0, n_full, body, 0)
    tile(n_full, True)
    lam = _lambda(lq1, lk1, lq2, lk2, lam_init)
    for hg in range(H_A):
        o = acc[hg] / _row_sum(l_s[hg], online)
        o_ref[:, hg * LANES:(hg + 1) * LANES] = o[:tq] - lam * o[tq:]


def _prompt_diff_attn(lams, qa, ka, va, lam_init, tq, tk, online):
    t = qa.shape[0]
    assert tk % tq == 0
    kern = functools.partial(_pdiff_kernel, tq=tq, tk=tk, lam_init=lam_init, online=online)
    return pl.pallas_call(
        kern,
        out_shape=jax.ShapeDtypeStruct((t, H_A * DV_A), F32),
        grid=(t // tq,),
        in_specs=[_const_spec((1, HD_A))] * 4 + [pl.BlockSpec((tq, W_QA), lambda i: (i, 0)),
                                                  _const_spec(ka.shape), _const_spec(va.shape)],
        out_specs=pl.BlockSpec((tq, H_A * DV_A), lambda i: (i, 0)),
        scratch_shapes=[pltpu.VMEM((H_A, 2 * tq, LANES), BF16), pltpu.VMEM((H_A, 2 * tq, LANES), F32),
                        pltpu.VMEM((H_A, 2 * tq, LANES), F32), pltpu.VMEM((H_A, 2 * tq, LANES), F32)],
        compiler_params=_cparams(("parallel",)),
        name="prompt_diff_attn",
    )(*lams, qa, ka, va)


def _sortable_key(score):
    bits = pltpu.bitcast(score, I32)
    return bits ^ ((bits >> 31) & 0x7FFFFFFF)


I16_MIN, I16_MAX = -(2 ** 15), 2 ** 15 - 1


def _kth_largest_half(halves, nt, rows, tk, topk):
    width = min(tk, 2 * LANES)

    def count_ge(cand):
        c16 = cand.astype(I16)

        def body(j, cnt):
            hit = jnp.where(halves[j] >= c16, jnp.int16(1), jnp.int16(0))
            for c in range(tk // width):
                cnt = cnt + hit[:, c * width:(c + 1) * width]
            return cnt

        cnt = lax.fori_loop(0, nt, body, jnp.zeros((rows, width), I16))
        return jnp.sum(cnt.astype(I32), axis=1, keepdims=True)

    zero = jnp.zeros((rows, 1), I32)
    t = jnp.where(count_ge(zero) >= topk, zero, jnp.full((rows, 1), I16_MIN, I32))

    def body(b, t):
        cand = t | jnp.left_shift(jnp.int32(1), 14 - b)
        return jnp.where(count_ge(cand) >= topk, cand, t)

    t = lax.fori_loop(0, 15, body, t)
    return t, count_ge(t)


def _drop_late_ties(keys, nt, rows, tk, thr, excess):
    col0 = lax.broadcasted_iota(I32, (rows, tk), 1)

    def count(pred):
        def body(j, cnt):
            hit = pred(keys[j], col0 + j * tk).astype(I32)
            for c in range(tk // LANES):
                cnt = cnt + hit[:, c * LANES:(c + 1) * LANES]
            return cnt
        return jnp.sum(lax.fori_loop(0, nt, body, jnp.zeros((rows, LANES), I32)), axis=1, keepdims=True)

    keep = count(lambda k, c: k == thr) - excess
    nbits = max(1, (keys.shape[0] * tk - 1).bit_length())

    def body(b, last):
        cand = last | jnp.left_shift(jnp.int32(1), nbits - 1 - b)
        before = count(lambda k, c: (k == thr) & (c < cand))
        return jnp.where(before <= keep - 1, cand, last)

    last = lax.fori_loop(0, nbits, body, jnp.zeros((rows, 1), I32))

    def drop(j, carry):
        k = keys[j]
        keys[j] = jnp.where((k == thr) & (col0 + j * tk > last), INT_MIN, k)
        return carry

    lax.fori_loop(0, nt, drop, 0)


def _topk_threshold(keys, hi16, lo16, nt, rows, tk, topk):
    top, n_top = _kth_largest_half(hi16, nt, rows, tk, topk)

    def low_tile(j, carry):
        key = keys[j]
        hi = key >> 16
        lo = (key & 0xFFFF) + I16_MIN
        lo16[j] = jnp.where(hi == top, lo, jnp.where(hi > top, I16_MAX, I16_MIN)).astype(I16)
        return carry

    lax.fori_loop(0, nt, low_tile, 0)
    low, n_low = _kth_largest_half(lo16, nt, rows, tk, topk)
    raw = top * 65536 + (low - I16_MIN)
    thr = jnp.maximum(raw, INT_MIN + 1)
    n_ge = jnp.where(low == I16_MIN, n_top, n_low)
    excess = jnp.where(raw == INT_MIN, 0, n_ge - topk)

    @pl.when(jnp.max(excess) > 0)
    def _():
        _drop_late_ties(keys, nt, rows, tk, thr, excess)

    return thr


def _pdsa_kernel(qi_ref, wi_ref, qb_ref, ki_ref, kb_ref, vb_ref, o_ref, keys, hi16, lo16, m_s, l_s, acc,
                 *, tq, tk, topk, online):
    i = pl.program_id(0)
    nt = (i * tq + tq + tk - 1) // tk
    row = i * tq + lax.broadcasted_iota(I32, (tq, tk), 0)
    col0 = lax.broadcasted_iota(I32, (tq, tk), 1)

    qi_st = jnp.concatenate([qi_ref[:, h * D_IDX:(h + 1) * D_IDX] for h in range(H_IDX)], axis=0)
    wv = wi_ref[...]
    w_cols = [wv[:, D_IDX + h:D_IDX + h + 1] for h in range(H_IDX)]

    def score_tile(j):
        start = pl.multiple_of(j * tk, tk)
        lg = _dot_t(qi_st, ki_ref[pl.ds(start, tk), :])
        sc = jnp.zeros((tq, tk), F32)
        for h in range(H_IDX):
            sc = sc + jnp.maximum(lg[h * tq:(h + 1) * tq], 0.0) * w_cols[h]
        key = jnp.where(col0 + j * tk <= row, _sortable_key(sc), INT_MIN)
        keys[j] = key
        hi16[j] = (key >> 16).astype(I16)

    _for_each_tile(nt, score_tile)

    thr = _topk_threshold(keys, hi16, lo16, nt, tq, tk, topk)

    m_s[...] = jnp.full(m_s.shape, NEG_BIG, F32)
    l_s[...] = jnp.zeros(l_s.shape, F32)
    acc[...] = jnp.zeros(acc.shape, F32)
    q_st = [jnp.concatenate([qb_ref[:, (kv * G_B + g) * HD_B:(kv * G_B + g + 1) * HD_B] for g in range(G_B)], axis=0)
            for kv in range(KV_B)]

    def attn_tile(j):
        start = pl.multiple_of(j * tk, tk)
        kt = keys[j]
        sel = kt >= thr
        sel = jnp.concatenate([sel] * G_B, axis=0)
        for kv in range(KV_B):
            s = _dot_t(q_st[kv], kb_ref[pl.ds(start, tk), kv * HD_B:(kv + 1) * HD_B])
            vt = vb_ref[pl.ds(start, tk), kv * HD_B:(kv + 1) * HD_B]
            if online:
                _online_softmax_step(jnp.where(sel, s, -jnp.inf), vt, m_s, l_s, acc, kv)
            else:
                _bounded_softmax_step(s, sel, vt, l_s, acc, kv)

    _for_each_tile(nt, attn_tile)
    for kv in range(KV_B):
        o = acc[kv] / _row_sum(l_s[kv], online)
        for g in range(G_B):
            hg = kv * G_B + g
            o_ref[:, hg * HD_B:(hg + 1) * HD_B] = o[g * tq:(g + 1) * tq]


def _prompt_dsa(qi, wi, qb, ki, kb, vb, tq, tk, online):
    t = qb.shape[0]
    topk = min(TOPK_MAX, t // 4)
    kern = functools.partial(_pdsa_kernel, tq=tq, tk=tk, topk=topk, online=online)
    row = lambda w: pl.BlockSpec((tq, w), lambda i: (i, 0))
    return pl.pallas_call(
        kern,
        out_shape=jax.ShapeDtypeStruct((t, H_B * HD_B), F32),
        grid=(t // tq,),
        in_specs=[row(W_QI), row(LANES), row(W_QB), _const_spec(ki.shape), _const_spec(kb.shape),
                  _const_spec(vb.shape)],
        out_specs=row(H_B * HD_B),
        scratch_shapes=[pltpu.VMEM((t // tk, tq, tk), I32), pltpu.VMEM((t // tk, tq, tk), I16),
                        pltpu.VMEM((t // tk, tq, tk), I16), pltpu.VMEM((KV_B, G_B * tq, LANES), F32),
                        pltpu.VMEM((KV_B, G_B * tq, LANES), F32), pltpu.VMEM((KV_B, G_B * tq, LANES), F32)],
        compiler_params=_cparams(("parallel",)),
        name="prompt_dsa",
    )(qi, wi, qb, ki, kb, vb)


def _fetch_pages(pt_ref, pools, bufs, sems, *, pps, nsteps, total):
    t = pl.program_id(0) * nsteps + pl.program_id(1)
    slot = t % 2

    def copies(tt, s, p):
        pg = pt_ref[tt // nsteps, (tt % nsteps) * pps + p]
        return [pltpu.make_async_copy(pool.at[pg], buf.at[s, p], sem.at[s])
                for pool, buf, sem in zip(pools, bufs, sems)]

    def start(tt, s):
        def body(p, carry):
            for cp in copies(tt, s, p):
                cp.start()
            return carry
        lax.fori_loop(0, pps, body, 0)

    @pl.when(t == 0)
    def _():
        start(t, slot)

    @pl.when(t + 1 < total)
    def _():
        start(t + 1, 1 - slot)

    def wait(p, carry):
        for cp in copies(t, slot, p):
            cp.wait()
        return carry

    lax.fori_loop(0, pps, wait, 0)
    return slot


def _page_scratch(pools, pps):
    bufs = [pltpu.VMEM((2, pps) + pool.shape[1:], pool.dtype) for pool in pools]
    sems = [pltpu.SemaphoreType.DMA((2,)) for _ in pools]
    return bufs + sems


def _sdiff_kernel(pt_ref, lq1, lk1, lq2, lk2, q_ref, kn_ref, vn_ref, kpool, vpool, o_ref,
                  kbuf, vbuf, ksem, vsem, kc, vc, lhs, m_s, l_s, acc, *, pps, page, tn, lam_init, nsteps, total):
    j = pl.program_id(1)

    @pl.when(j == 0)
    def _():
        q = q_ref[0].astype(F32)
        for kv in range(KV_A):
            parts = [_split_components(q[:, (kv * G_A + g) * LANES:(kv * G_A + g + 1) * LANES]) for g in range(G_A)]
            lhs[kv] = jnp.concatenate(parts, axis=0).astype(BF16)
        m_s[...] = jnp.full(m_s.shape, NEG_BIG, F32)
        l_s[...] = jnp.zeros(l_s.shape, F32)
        acc[...] = jnp.zeros(acc.shape, F32)

    slot = _fetch_pages(pt_ref, [kpool, vpool], [kbuf, vbuf], [ksem, vsem], pps=pps, nsteps=nsteps, total=total)
    for p in range(pps):
        for kv in range(KV_A):
            kc[kv, :, p * page:(p + 1) * page] = kbuf[slot, p, kv].reshape(2 * HD_A, page).astype(BF16)
            vc[kv, p * page:(p + 1) * page, :] = vbuf[slot, p, pl.ds(kv, page, stride=KV_A), :].astype(BF16)
    for kv in range(KV_A):
        s = _dot(lhs[kv], kc[kv])
        _online_softmax_step(s, vc[kv], m_s, l_s, acc, kv)

    @pl.when(j == pl.num_programs(1) - 1)
    def _():
        kn = kn_ref[0]
        vn = vn_ref[0]
        lam = _lambda(lq1, lk1, lq2, lk2, lam_init)
        for kv in range(KV_A):
            s = _dot_t(lhs[kv], kn[:, kv * LANES:(kv + 1) * LANES])
            r = lax.broadcasted_iota(I32, s.shape, 0) % tn
            c = lax.broadcasted_iota(I32, s.shape, 1)
            s = jnp.where(c <= r, s, -jnp.inf)
            _online_softmax_step(s, vn[:, kv * LANES:(kv + 1) * LANES], m_s, l_s, acc, kv)
            o = acc[kv] / l_s[kv]
            for g in range(G_A):
                hg = kv * G_A + g
                o_ref[0, :, hg * LANES:(hg + 1) * LANES] = (o[(2 * g) * tn:(2 * g + 1) * tn]
                                                            - lam * o[(2 * g + 1) * tn:(2 * g + 2) * tn])


def _sample_diff_attn(page_table, lams, qa, ka_new, va_new, pool_kt, pool_v, lam_init, pps):
    nb, tn, _ = qa.shape
    n_pages = page_table.shape[1]
    page = pool_kt.shape[-1]
    nsteps = n_pages // pps
    kern = functools.partial(_sdiff_kernel, pps=pps, page=page, tn=tn, lam_init=lam_init, nsteps=nsteps,
                             total=nb * nsteps)
    seq = lambda w: pl.BlockSpec((1, tn, w), lambda b, j, pt: (b, 0, 0))
    cst = lambda shape: pl.BlockSpec(shape, lambda b, j, pt: (0,) * len(shape))
    hbm = pl.BlockSpec(memory_space=pl.ANY)
    rows = 2 * G_A * tn
    return pl.pallas_call(
        kern,
        out_shape=jax.ShapeDtypeStruct((nb, tn, H_A * DV_A), F32),
        grid_spec=pltpu.PrefetchScalarGridSpec(
            num_scalar_prefetch=1,
            grid=(nb, nsteps),
            in_specs=[cst((1, HD_A))] * 4 + [seq(W_QA), seq(W_KA), seq(W_VA), hbm, hbm],
            out_specs=seq(H_A * DV_A),
            scratch_shapes=_page_scratch([pool_kt, pool_v], pps)
                           + [pltpu.VMEM((KV_A, 2 * HD_A, pps * page), BF16),
                              pltpu.VMEM((KV_A, pps * page, DV_A), BF16),
                              pltpu.VMEM((KV_A, rows, LANES), BF16), pltpu.VMEM((KV_A, rows, LANES), F32),
                              pltpu.VMEM((KV_A, rows, LANES), F32), pltpu.VMEM((KV_A, rows, LANES), F32)]),
        compiler_params=_cparams(("arbitrary", "arbitrary")),
        name="sample_diff_attn",
    )(page_table, *lams, qa, ka_new, va_new, pool_kt, pool_v)


def _sscore_kernel(pt_ref, qi_ref, wi_ref, kin_ref, ipool, keys_o, newkeys_o, ibuf, isem,
                   *, pps, page, tn, tile, nsteps, total):
    j = pl.program_id(1)
    q = qi_ref[0].astype(F32)
    qi_st = jnp.concatenate([q[:, h * D_IDX:(h + 1) * D_IDX] for h in range(H_IDX)], axis=0).astype(BF16)
    wv = wi_ref[0]
    w_cols = [wv[:, D_IDX + h:D_IDX + h + 1] for h in range(H_IDX)]

    def scores(lg):
        sc = jnp.zeros((tn, lg.shape[1]), F32)
        for h in range(H_IDX):
            sc = sc + jnp.maximum(lg[h * tn:(h + 1) * tn], 0.0) * w_cols[h]
        return _sortable_key(sc)

    slot = _fetch_pages(pt_ref, [ipool], [ibuf], [isem], pps=pps, nsteps=nsteps, total=total)
    kt = jnp.concatenate([ibuf[slot, p] for p in range(pps)], axis=1).astype(BF16)
    keys = scores(_dot(qi_st, kt))
    for c in range(pps * page // tile):
        keys_o[c] = keys[:, c * tile:(c + 1) * tile]

    @pl.when(j == nsteps - 1)
    def _():
        nk = scores(_dot_t(qi_st, kin_ref[0]))
        r = lax.broadcasted_iota(I32, nk.shape, 0)
        c = lax.broadcasted_iota(I32, nk.shape, 1)
        nk = jnp.where(c <= r, nk, INT_MIN)
        newkeys_o[...] = jnp.concatenate([nk, jnp.full((tn, LANES - tn), INT_MIN, I32)], axis=1)


def _sample_scores(page_table, qi, wi, ki_new, pool_it, pps, tile):
    nb, tn, _ = qi.shape
    n_pages = page_table.shape[1]
    page = pool_it.shape[2]
    nsteps = n_pages // pps
    tiles_per_step = pps * page // tile
    kern = functools.partial(_sscore_kernel, pps=pps, page=page, tn=tn, tile=tile, nsteps=nsteps,
                             total=nb * nsteps)
    seq = lambda w: pl.BlockSpec((1, tn, w), lambda b, j, pt: (b, 0, 0))
    return pl.pallas_call(
        kern,
        out_shape=[jax.ShapeDtypeStruct((n_pages * page // tile, nb * tn, tile), I32),
                   jax.ShapeDtypeStruct((nb * tn, LANES), I32)],
        grid_spec=pltpu.PrefetchScalarGridSpec(
            num_scalar_prefetch=1,
            grid=(nb, nsteps),
            in_specs=[seq(W_QI), seq(LANES), seq(D_IDX), pl.BlockSpec(memory_space=pl.ANY)],
            out_specs=[pl.BlockSpec((tiles_per_step, tn, tile), lambda b, j, pt: (j, b, 0)),
                       pl.BlockSpec((tn, LANES), lambda b, j, pt: (b, 0))],
            scratch_shapes=_page_scratch([pool_it], pps)),
        compiler_params=_cparams(("arbitrary", "arbitrary")),
        name="sample_scores",
    )(page_table, qi, wi, ki_new, pool_it)


def _sthr_kernel(keys_ref, newkeys_ref, sel_ref, keys, hi16, lo16, *, rows, tile, nt, topk):
    for j in range(nt - 1):
        keys[j] = keys_ref[j]
    pad = [jnp.full((rows, tile - LANES), INT_MIN, I32)] if tile > LANES else []
    keys[nt - 1] = jnp.concatenate([newkeys_ref[...]] + pad, axis=1)

    def split(j, carry):
        hi16[j] = (keys[j] >> 16).astype(I16)
        return carry

    lax.fori_loop(0, nt, split, 0)
    thr = _topk_threshold(keys, hi16, lo16, nt, rows, tile, topk)
    for j in range(nt - 1):
        sel_ref[:, j * tile:(j + 1) * tile] = (keys[j] >= thr).astype(F32)
    sel_ref[:, (nt - 1) * tile:] = (keys[nt - 1][:, :LANES] >= thr).astype(F32)


def _sample_select(keys, newkeys, topk):
    n_tiles, rows_all, tile = keys.shape
    rows = _tile(rows_all, 128)
    nt = n_tiles + 1
    kern = functools.partial(_sthr_kernel, rows=rows, tile=tile, nt=nt, topk=topk)
    width = n_tiles * tile + LANES
    return pl.pallas_call(
        kern,
        out_shape=jax.ShapeDtypeStruct((rows_all, width), F32),
        grid=(rows_all // rows,),
        in_specs=[pl.BlockSpec((n_tiles, rows, tile), lambda i: (0, i, 0)),
                  pl.BlockSpec((rows, LANES), lambda i: (i, 0))],
        out_specs=pl.BlockSpec((rows, width), lambda i: (i, 0)),
        scratch_shapes=[pltpu.VMEM((nt, rows, tile), I32), pltpu.VMEM((nt, rows, tile), I16),
                        pltpu.VMEM((nt, rows, tile), I16)],
        compiler_params=_cparams(("parallel",)),
        name="sample_select",
    )(keys, newkeys)


def _sdsa_kernel(pt_ref, q_ref, sel_ref, seln_ref, kn_ref, vn_ref, kpool, vpool, o_ref,
                 kbuf, vbuf, ksem, vsem, kc, vc, lhs, m_s, l_s, acc, *, pps, page, tn, nsteps, total):
    j = pl.program_id(1)

    @pl.when(j == 0)
    def _():
        q = q_ref[0].astype(F32)
        for kv in range(KV_B):
            parts = [q[:, (kv * G_B + g) * HD_B:(kv * G_B + g + 1) * HD_B] for g in range(G_B)]
            lhs[kv] = jnp.concatenate(parts, axis=0).astype(BF16)
        m_s[...] = jnp.full(m_s.shape, NEG_BIG, F32)
        l_s[...] = jnp.zeros(l_s.shape, F32)
        acc[...] = jnp.zeros(acc.shape, F32)

    def step(sel, kmats, vmats):
        sel = jnp.concatenate([sel > 0.5] * G_B, axis=0)
        for kv in range(KV_B):
            s = jnp.where(sel, _dot_t(lhs[kv], kmats[kv]), -jnp.inf)
            _online_softmax_step(s, vmats[kv], m_s, l_s, acc, kv)

    slot = _fetch_pages(pt_ref, [kpool, vpool], [kbuf, vbuf], [ksem, vsem], pps=pps, nsteps=nsteps, total=total)
    for p in range(pps):
        for kv in range(KV_B):
            kc[kv, p * page:(p + 1) * page, :] = kbuf[slot, p, pl.ds(kv, page, stride=KV_B), :].astype(BF16)
            vc[kv, p * page:(p + 1) * page, :] = vbuf[slot, p, pl.ds(kv, page, stride=KV_B), :].astype(BF16)
    step(sel_ref[0], [kc[kv] for kv in range(KV_B)], [vc[kv] for kv in range(KV_B)])

    @pl.when(j == nsteps - 1)
    def _():
        kn, vn = kn_ref[0], vn_ref[0]
        step(seln_ref[0][:, :tn], [kn[:, kv * HD_B:(kv + 1) * HD_B] for kv in range(KV_B)],
             [vn[:, kv * HD_B:(kv + 1) * HD_B] for kv in range(KV_B)])
        for kv in range(KV_B):
            o = acc[kv] / l_s[kv]
            for g in range(G_B):
                hg = kv * G_B + g
                o_ref[0, :, hg * HD_B:(hg + 1) * HD_B] = o[g * tn:(g + 1) * tn]


def _sample_dsa(page_table, qb, sel, kb_new, vb_new, pool_k, pool_v, pps):
    nb, tn, _ = qb.shape
    n_pages = page_table.shape[1]
    page = pool_k.shape[1] // KV_B
    tk = pps * page
    nsteps = n_pages // pps
    kern = functools.partial(_sdsa_kernel, pps=pps, page=page, tn=tn, nsteps=nsteps, total=nb * nsteps)
    seq = lambda w: pl.BlockSpec((1, tn, w), lambda b, j, pt: (b, 0, 0))
    hbm = pl.BlockSpec(memory_space=pl.ANY)
    rows = G_B * tn
    return pl.pallas_call(
        kern,
        out_shape=jax.ShapeDtypeStruct((nb, tn, H_B * HD_B), F32),
        grid_spec=pltpu.PrefetchScalarGridSpec(
            num_scalar_prefetch=1,
            grid=(nb, nsteps),
            in_specs=[seq(W_QB), pl.BlockSpec((1, tn, tk), lambda b, j, pt: (b, 0, j)),
                      pl.BlockSpec((1, tn, LANES), lambda b, j, pt: (b, 0, (n_pages * page) // LANES)),
                      seq(W_KB), seq(W_KB), hbm, hbm],
            out_specs=seq(H_B * HD_B),
            scratch_shapes=_page_scratch([pool_k, pool_v], pps)
                           + [pltpu.VMEM((KV_B, tk, HD_B), BF16), pltpu.VMEM((KV_B, tk, HD_B), BF16),
                              pltpu.VMEM((KV_B, rows, LANES), BF16), pltpu.VMEM((KV_B, rows, LANES), F32),
                              pltpu.VMEM((KV_B, rows, LANES), F32), pltpu.VMEM((KV_B, rows, LANES), F32)]),
        compiler_params=_cparams(("arbitrary", "arbitrary")),
        name="sample_dsa",
    )(page_table, qb, sel, sel, kb_new, vb_new, pool_k, pool_v)


def _finish_kernel(x_ref, oa_ref, ob_ref, ga_ref, gb_ref, sub_ref, gm_ref, wa, wb, wo, wu, wd, y_ref, *, lam_init):
    oa = oa_ref[...]
    sub = sub_ref[...]
    chunks = []
    for h in range(H_A):
        c = oa[:, h * DV_A:(h + 1) * DV_A]
        ms = jnp.mean(c * c, axis=-1, keepdims=True)
        chunks.append(((c * lax.rsqrt(ms + EPS)) * sub) * (1.0 - lam_init))
    oan = jnp.concatenate(chunks, axis=1).astype(BF16)
    pa = _dot(oan, wa[...])
    pb = _dot(ob_ref[...].astype(BF16), wb[...])
    mix = (ga_ref[...] * pa + gb_ref[...] * pb).astype(BF16)
    x1 = x_ref[...] + _dot(mix, wo[...])
    ms = jnp.mean(x1 * x1, axis=-1, keepdims=True)
    xn = ((x1 * lax.rsqrt(ms + EPS)) * gm_ref[...]).astype(BF16)
    hid = jnp.maximum(_dot(xn, wu[...]), 0.0)
    y_ref[...] = x1 + _dot((hid * hid).astype(BF16), wd[...])


def _finish(x, oa, ob, ga, gb, pw, lam_init, tm):
    t, d = x.shape
    row = pl.BlockSpec((tm, d), lambda i: (i, 0))
    weights = [pw[k] for k in ("w_a_proj", "w_b_proj", "w_out", "w_up", "w_down")]
    kern = functools.partial(_finish_kernel, lam_init=lam_init)
    return pl.pallas_call(
        kern,
        out_shape=jax.ShapeDtypeStruct((t, d), F32),
        grid=(t // tm,),
        in_specs=[row] * 5 + [_const_spec((1, DV_A)), _const_spec((1, d))] + [_const_spec(w.shape) for w in weights],
        out_specs=row,
        compiler_params=_cparams(("parallel",)),
        name="finish",
    )(x, oa, ob, ga, gb, pw["sub_norm_a"], pw["g_mlp"], *weights)


def _prep_weights(l, g_attn_norm, w_in, q_norm_a, k_norm_a, sub_norm_a, q_norm_b, k_norm_b, idx_k_norm,
                  w_a_proj, w_b_proj, w_out, g_mlp_norm, w_up, w_down):
    w = w_in[l].astype(BF16)
    widths = (W_QA, W_KA, W_VA, W_QB, W_KB, W_KB, W_QI, D_IDX, H_IDX, w.shape[0], w.shape[0])
    offs = np.concatenate([[0], np.cumsum(widths)])
    seg = [w[:, int(offs[i]):int(offs[i + 1])] for i in range(len(widths))]
    wkw = jnp.concatenate([seg[7], seg[8], jnp.zeros((w.shape[0], LANES - D_IDX - H_IDX), BF16)], axis=1)
    lane = np.arange(LANES)
    row2 = lambda v: v.astype(F32)[None, :]
    return {
        "g_attn": row2(g_attn_norm[l]),
        "wqa": seg[0], "wka": seg[1], "wva": seg[2], "wqb": seg[3], "wkb": seg[4], "wvb": seg[5],
        "wqi": seg[6], "wkw": wkw, "wga": seg[9], "wgb": seg[10],
        "gqa": row2(jnp.tile(q_norm_a[l], W_QA // HD_A) * (HD_A ** -0.5 * LOG2E)),
        "gka": row2(jnp.tile(k_norm_a[l], W_KA // HD_A)),
        "gqb": row2(jnp.tile(q_norm_b[l], W_QB // HD_B) * (HD_B ** -0.5 * LOG2E)),
        "gkb": row2(jnp.tile(k_norm_b[l], W_KB // HD_B)),
        "gki": row2(jnp.concatenate([idx_k_norm[l], jnp.zeros((LANES - D_IDX,), F32)])),
        "ones64": jnp.asarray((lane[:, None] // HD_A) == (lane[None, :] // HD_A), BF16),
        "ones128": jnp.ones((LANES, LANES), BF16),
        "sub_norm_a": row2(sub_norm_a[l]), "g_mlp": row2(g_mlp_norm[l]),
        "w_a_proj": w_a_proj[l].astype(BF16), "w_b_proj": w_b_proj[l].astype(BF16),
        "w_out": w_out[l].astype(BF16), "w_up": w_up[l].astype(BF16), "w_down": w_down[l].astype(BF16),
    }


def _tile(n, pref):
    return pref if n % pref == 0 else n


def _tiles(seq, n_sample_rows, n_pages):
    pps = next(p for p in (32, 16, 8, 4, 2, 1) if n_pages % p == 0)
    return {"proj": _tile(seq, 256), "diff": _tile(seq, 256), "diff_k": _tile(seq, 512),
            "dsa_q": _tile(seq, 128), "dsa_k": _tile(seq, 512),
            "finish": _tile(seq, 256), "sample_rows": _tile(n_sample_rows, 256), "pps": pps,
            "sel_tile": _tile(pps * 128, 512)}


def _score_bound(q_gain, k_gain, head_dim):
    return (head_dim ** 0.5) * jnp.max(jnp.abs(q_gain)) * jnp.max(jnp.abs(k_gain))


def kernel(x_prompt, x_sample, cache_a_k, cache_a_v, cache_b_k, cache_b_v, cache_b_idx_k, page_table, g_attn_norm, w_in, q_norm_a, k_norm_a, lam_q1, lam_k1, lam_q2, lam_k2, sub_norm_a, q_norm_b, k_norm_b, idx_k_norm, w_a_proj, w_b_proj, w_out, g_mlp_norm, w_up, w_down):
    depth = w_in.shape[0]
    nb_p, seq, d = x_prompt.shape
    nb_s, tn, _ = x_sample.shape
    n_pages = page_table.shape[1]
    n_phys, page = cache_a_k.shape[1], cache_a_k.shape[2]
    past = n_pages * page
    assert nb_p == 1
    pos_p = jnp.arange(seq)
    pos_s = past + (jnp.arange(nb_s * tn) % tn)
    tl = _tiles(seq, nb_s * tn, n_pages)
    pps = tl["pps"]

    xp = x_prompt.reshape(seq, d)
    xs = x_sample.reshape(nb_s * tn, d)
    outs_p = [[] for _ in range(5)]
    outs_s = [[] for _ in range(5)]
    for l in range(depth):
        lam_init = 0.8 - 0.6 * math.exp(-0.3 * l)
        pw = _prep_weights(l, g_attn_norm, w_in, q_norm_a, k_norm_a, sub_norm_a, q_norm_b, k_norm_b, idx_k_norm,
                           w_a_proj, w_b_proj, w_out, g_mlp_norm, w_up, w_down)
        lams = [v[l].astype(F32)[None, :] for v in (lam_q1, lam_k1, lam_q2, lam_k2)]

        (qa, kaf, kab, vaf, vab, qb, kbf, kbb, vbf, vbb, qi, kif, kib, wi, ga, gb) = _project(
            xs, pos_s, pw, tl["sample_rows"])
        seq3 = lambda a: a.reshape(nb_s, tn, a.shape[-1])
        pool_akt = jnp.transpose(cache_a_k[l], (0, 2, 3, 4, 1))
        pool_it = jnp.transpose(cache_b_idx_k[l], (0, 2, 1))
        rows2 = lambda c: c[l].reshape(n_phys, page * c.shape[3], c.shape[4])
        oa = _sample_diff_attn(page_table, lams, seq3(qa), seq3(kab), seq3(vab), pool_akt, rows2(cache_a_v),
                               lam_init, pps)
        keys, newkeys = _sample_scores(page_table, seq3(qi), seq3(wi), seq3(kib), pool_it, pps, tl["sel_tile"])
        sel = seq3(_sample_select(keys, newkeys, min(TOPK_MAX, (past + tn) // 4)))
        ob = _sample_dsa(page_table, seq3(qb), sel, seq3(kbb), seq3(vbb), rows2(cache_b_k), rows2(cache_b_v), pps)
        xs = _finish(xs, oa.reshape(nb_s * tn, -1), ob.reshape(nb_s * tn, -1), ga, gb, pw, lam_init,
                     tl["sample_rows"])
        for lst, v in zip(outs_s, (kaf.reshape(nb_s, tn, KV_A, 2, HD_A), vaf.reshape(nb_s, tn, KV_A, DV_A),
                                   kbf.reshape(nb_s, tn, KV_B, HD_B), vbf.reshape(nb_s, tn, KV_B, HD_B),
                                   kif.reshape(nb_s, tn, D_IDX))):
            lst.append(v)

        (qa, kaf, kab, vaf, vab, qb, kbf, kbb, vbf, vbb, qi, kif, kib, wi, ga, gb) = _project(
            xp, pos_p, pw, tl["proj"])
        diff = lambda online: functools.partial(_prompt_diff_attn, lams, qa, kab, vab, lam_init, tl["diff"], tl["diff_k"],
                                                online)
        oa = lax.cond(_score_bound(q_norm_a[l], k_norm_a[l], HD_A) <= MAX_BOUNDED_SCORE, diff(False), diff(True))
        dsa = lambda online: functools.partial(_prompt_dsa, qi, wi, qb, kib, kbb, vbb, tl["dsa_q"], tl["dsa_k"],
                                               online)
        ob = lax.cond(_score_bound(q_norm_b[l], k_norm_b[l], HD_B) <= MAX_BOUNDED_SCORE, dsa(False), dsa(True))
        xp = _finish(xp, oa, ob, ga, gb, pw, lam_init, tl["finish"])
        for lst, v in zip(outs_p, (kaf.reshape(1, seq, KV_A, 2, HD_A), vaf.reshape(1, seq, KV_A, DV_A),
                                   kbf.reshape(1, seq, KV_B, HD_B), vbf.reshape(1, seq, KV_B, HD_B),
                                   kif.reshape(1, seq, D_IDX))):
            lst.append(v)

    return (xp.reshape(nb_p, seq, d), xs.reshape(nb_s, tn, d),
            *[jnp.stack(v) for v in outs_p], *[jnp.stack(v) for v in outs_s])
```

```python
import functools
import math

import jax
import jax.numpy as jnp
import numpy as np
from jax import lax
from jax.experimental import pallas as pl
from jax.experimental.pallas import tpu as pltpu

F32 = jnp.float32
BF16 = jnp.bfloat16
I32 = jnp.int32
I16 = jnp.int16

LANES = 128
VMEM_LIMIT = 56 * 1024 * 1024
EPS = 1e-6
ROPE_THETA = 10000.0
TOPK_MAX = 256
NEG_BIG = -1e30
INT_MIN = -(2 ** 31)
LOG2E = 1.4426950408889634
MAX_BOUNDED_SCORE = 60.0

H_A, KV_A, HD_A = 8, 2, 64
G_A = H_A // KV_A
DV_A = 2 * HD_A
H_B, KV_B, HD_B = 8, 2, 128
G_B = H_B // KV_B
H_IDX, D_IDX = 8, 64
W_QA = H_A * 2 * HD_A
W_KA = KV_A * 2 * HD_A
W_VA = KV_A * DV_A
W_QB = H_B * HD_B
W_KB = KV_B * HD_B
W_QI = H_IDX * D_IDX


def _cparams(sem):
    return pltpu.CompilerParams(dimension_semantics=sem, vmem_limit_bytes=VMEM_LIMIT)


def _const_spec(shape):
    nd = len(shape)
    return pl.BlockSpec(shape, lambda *_: (0,) * nd, pipeline_mode=pl.Buffered(1))


def _dot_t(a, b):
    return lax.dot_general(a, b, (((1,), (1,)), ((), ())), preferred_element_type=F32)


def _dot(a, b):
    return jnp.dot(a, b, preferred_element_type=F32)


def _group_rms(h, ones_ref, inv_group):
    outs = []
    for c in range(h.shape[1] // LANES):
        hs = h[:, c * LANES:(c + 1) * LANES]
        sq = hs * hs
        hi = sq.astype(BF16)
        lo = (sq - hi.astype(F32)).astype(BF16)
        ss = _dot(hi, ones_ref[...]) + _dot(lo, ones_ref[...])
        outs.append(hs * lax.rsqrt(ss * inv_group + EPS))
    return outs


def _rope_chunks(chunks, gain, cos, sin, half):
    lane = lax.broadcasted_iota(I32, chunks[0].shape, 1)
    first = (lane % (2 * half)) < half
    outs = []
    for c, y in enumerate(chunks):
        y = y * gain[:, c * LANES:(c + 1) * LANES]
        if half == LANES // 2:
            partner = pltpu.roll(y, half, 1)
        else:
            partner = jnp.where(first, pltpu.roll(y, LANES - half, 1), pltpu.roll(y, half, 1))
        outs.append(y * cos + partner * sin)
    return outs


def _proj_kernel(x_ref, g_ref, wqa, wka, wva, wqb, wkb, wvb, wqi, wkw, wga, wgb,
                 gqa, gka, gqb, gkb, gki, cos_a, sin_a, cos_b, sin_b, ones64, ones128,
                 qa_o, kaf_o, kab_o, vaf_o, vab_o, qb_o, kbf_o, kbb_o, vbf_o, vbb_o,
                 qi_o, kif_o, kib_o, wi_o, ga_o, gb_o):
    x = x_ref[...]
    ms = jnp.mean(x * x, axis=-1, keepdims=True)
    xn = ((x * lax.rsqrt(ms + EPS)) * g_ref[...]).astype(BF16)
    ca, sa, cb, sb = cos_a[...], sin_a[...], cos_b[...], sin_b[...]

    def cat(chunks):
        return chunks[0] if len(chunks) == 1 else jnp.concatenate(chunks, axis=1)

    h = _dot(xn, wqa[...])
    qa_o[...] = cat(_rope_chunks(_group_rms(h, ones64, 1.0 / HD_A), gqa[...], ca, sa, HD_A // 2)).astype(BF16)
    h = _dot(xn, wka[...])
    ka = cat(_rope_chunks(_group_rms(h, ones64, 1.0 / HD_A), gka[...], ca, sa, HD_A // 2))
    kaf_o[...] = ka
    kab_o[...] = ka.astype(BF16)
    h = _dot(xn, wva[...])
    vaf_o[...] = h
    vab_o[...] = h.astype(BF16)
    h = _dot(xn, wqb[...])
    qb_o[...] = cat(_rope_chunks(_group_rms(h, ones128, 1.0 / HD_B), gqb[...], cb, sb, HD_B // 2)).astype(BF16)
    h = _dot(xn, wkb[...])
    kb = cat(_rope_chunks(_group_rms(h, ones128, 1.0 / HD_B), gkb[...], cb, sb, HD_B // 2))
    kbf_o[...] = kb
    kbb_o[...] = kb.astype(BF16)
    h = _dot(xn, wvb[...])
    vbf_o[...] = h
    vbb_o[...] = h.astype(BF16)
    h = _dot(xn, wqi[...])
    chunks = [h[:, c * LANES:(c + 1) * LANES] for c in range(W_QI // LANES)]
    ones_gain = jnp.full((1, W_QI), D_IDX ** -0.5, F32)
    qi_o[...] = cat(_rope_chunks(chunks, ones_gain, ca, sa, D_IDX // 2)).astype(BF16)
    h = _dot(xn, wkw[...])
    ki = _rope_chunks(_group_rms(h, ones64, 1.0 / D_IDX), gki[...], ca, sa, D_IDX // 2)[0]
    kif_o[...] = ki[:, :D_IDX]
    kib_o[...] = ki[:, :D_IDX].astype(BF16)
    wi_o[...] = h * (H_IDX ** -0.5)
    ga_o[...] = jax.nn.sigmoid(_dot(xn, wga[...]))
    gb_o[...] = jax.nn.sigmoid(_dot(xn, wgb[...]))


def _rope_tables(pos, half, group):
    inv_freq = ROPE_THETA ** (-jnp.arange(half, dtype=F32) / half)
    ang = pos.astype(F32)[:, None] * inv_freq[None, :]
    cos, sin = jnp.cos(ang), jnp.sin(ang)
    reps = LANES // group
    cos_t = jnp.tile(jnp.concatenate([cos, cos], axis=1), (1, reps))
    sin_t = jnp.tile(jnp.concatenate([-sin, sin], axis=1), (1, reps))
    return cos_t, sin_t


def _project(x, pos, pw, tm):
    t, d = x.shape
    cos_a, sin_a = _rope_tables(pos, HD_A // 2, HD_A)
    cos_b, sin_b = _rope_tables(pos, HD_B // 2, HD_B)
    row = lambda w: pl.BlockSpec((tm, w), lambda i: (i, 0))
    weights = [pw[k] for k in ("wqa", "wka", "wva", "wqb", "wkb", "wvb", "wqi", "wkw", "wga", "wgb")]
    consts = [pw[k] for k in ("gqa", "gka", "gqb", "gkb", "gki")]
    in_specs = ([row(d), _const_spec((1, d))] + [_const_spec(w.shape) for w in weights]
                + [_const_spec(c.shape) for c in consts] + [row(LANES)] * 4
                + [_const_spec((LANES, LANES))] * 2)
    outs = [(W_QA, BF16), (W_KA, F32), (W_KA, BF16), (W_VA, F32), (W_VA, BF16),
            (W_QB, BF16), (W_KB, F32), (W_KB, BF16), (W_KB, F32), (W_KB, BF16),
            (W_QI, BF16), (D_IDX, F32), (D_IDX, BF16), (LANES, F32), (d, F32), (d, F32)]
    return pl.pallas_call(
        _proj_kernel,
        out_shape=[jax.ShapeDtypeStruct((t, w), dt) for w, dt in outs],
        grid=(t // tm,),
        in_specs=in_specs,
        out_specs=[row(w) for w, _ in outs],
        compiler_params=_cparams(("parallel",)),
        name="proj",
    )(x, pw["g_attn"], *weights, *consts, cos_a, sin_a, cos_b, sin_b, pw["ones64"], pw["ones128"])


def _lane_rep(col):
    return jnp.broadcast_to(col, (col.shape[0], LANES))


def _online_softmax_step(s, v, m_ref, l_ref, acc_ref, idx):
    m_old = m_ref[idx]
    m_new = jnp.maximum(m_old, _lane_rep(jnp.max(s, axis=1, keepdims=True)))
    alpha = jnp.exp2(m_old - m_new)
    p = jnp.exp2(s - m_new[:, :1])
    l_ref[idx] = alpha * l_ref[idx] + _lane_rep(jnp.sum(p, axis=1, keepdims=True))
    acc_ref[idx] = alpha * acc_ref[idx] + _dot(p.astype(BF16), v)
    m_ref[idx] = m_new


def _bounded_softmax_step(s, keep, v, l_ref, acc_ref, idx):
    p = jnp.exp2(s)
    if keep is not None:
        p = jnp.where(keep, p, 0.0)
    part = p[:, :LANES]
    for c in range(1, p.shape[1] // LANES):
        part = part + p[:, c * LANES:(c + 1) * LANES]
    l_ref[idx] = l_ref[idx] + part
    acc_ref[idx] = acc_ref[idx] + _dot(p.astype(BF16), v)


def _row_sum(l, online):
    return l[:, :1] if online else jnp.sum(l, axis=1, keepdims=True)


TILES_PER_TRIP = 4


def _for_each_tile(nt, fn):
    def group(jj, carry):
        for u in range(TILES_PER_TRIP):
            fn(TILES_PER_TRIP * jj + u)
        return carry

    def single(j, carry):
        fn(j)
        return carry

    lax.fori_loop(0, nt // TILES_PER_TRIP, group, 0)
    lax.fori_loop(nt - nt % TILES_PER_TRIP, nt, single, 0)


def _split_components(qg):
    lane = lax.broadcasted_iota(I32, qg.shape, 1)
    zero = jnp.zeros_like(qg)
    return jnp.concatenate([jnp.where(lane < HD_A, qg, zero), jnp.where(lane >= HD_A, qg, zero)], axis=0)


def _lambda(lq1, lk1, lq2, lk2, lam_init):
    a = jnp.exp(jnp.sum(lq1[...] * lk1[...], axis=1, keepdims=True))
    b = jnp.exp(jnp.sum(lq2[...] * lk2[...], axis=1, keepdims=True))
    return a - b + lam_init


def _pdiff_kernel(lq1, lk1, lq2, lk2, q_ref, k_ref, v_ref, o_ref, lhs, m_s, l_s, acc,
                  *, tq, tk, lam_init, online):
    i = pl.program_id(0)
    for hg in range(H_A):
        lhs[hg] = _split_components(q_ref[:, hg * LANES:(hg + 1) * LANES])
    m_s[...] = jnp.full(m_s.shape, NEG_BIG, F32)
    l_s[...] = jnp.zeros(l_s.shape, F32)
    acc[...] = jnp.zeros(acc.shape, F32)

    def tile(j, masked):
        start = pl.multiple_of(j * tk, tk)
        for kv in range(KV_A):
            kt = k_ref[pl.ds(start, tk), kv * LANES:(kv + 1) * LANES]
            vt = v_ref[pl.ds(start, tk), kv * LANES:(kv + 1) * LANES]
            for g in range(G_A):
                hg = kv * G_A + g
                s = _dot_t(lhs[hg], kt)
                keep = None
                if masked:
                    r = i * tq + lax.broadcasted_iota(I32, s.shape, 0) % tq
                    c = j * tk + lax.broadcasted_iota(I32, s.shape, 1)
                    keep = c <= r
                if online:
                    if masked:
                        s = jnp.where(keep, s, -jnp.inf)
                    _online_softmax_step(s, vt, m_s, l_s, acc, hg)
                else:
                    _bounded_softmax_step(s, keep, vt, l_s, acc, hg)

    def body(j, carry):
        tile(j, False)
        return carry

    n_full = (i * tq) // tk
    lax.fori_loop(0, n_full, body, 0)
    tile(n_full, True)
    lam = _lambda(lq1, lk1, lq2, lk2, lam_init)
    for hg in range(H_A):
        o = acc[hg] / _row_sum(l_s[hg], online)
        o_ref[:, hg * LANES:(hg + 1) * LANES] = o[:tq] - lam * o[tq:]


def _prompt_diff_attn(lams, qa, ka, va, lam_init, tq, tk, online):
    t = qa.shape[0]
    assert tk % tq == 0
    kern = functools.partial(_pdiff_kernel, tq=tq, tk=tk, lam_init=lam_init, online=online)
    return pl.pallas_call(
        kern,
        out_shape=jax.ShapeDtypeStruct((t, H_A * DV_A), F32),
        grid=(t // tq,),
        in_specs=[_const_spec((1, HD_A))] * 4 + [pl.BlockSpec((tq, W_QA), lambda i: (i, 0)),
                                                  _const_spec(ka.shape), _const_spec(va.shape)],
        out_specs=pl.BlockSpec((tq, H_A * DV_A), lambda i: (i, 0)),
        scratch_shapes=[pltpu.VMEM((H_A, 2 * tq, LANES), BF16), pltpu.VMEM((H_A, 2 * tq, LANES), F32),
                        pltpu.VMEM((H_A, 2 * tq, LANES), F32), pltpu.VMEM((H_A, 2 * tq, LANES), F32)],
        compiler_params=_cparams(("parallel",)),
        name="prompt_diff_attn",
    )(*lams, qa, ka, va)


def _sortable_key(score):
    bits = pltpu.bitcast(score, I32)
    return bits ^ ((bits >> 31) & 0x7FFFFFFF)


I16_MIN, I16_MAX = -(2 ** 15), 2 ** 15 - 1


def _kth_largest_half(halves, nt, rows, tk, topk):
    width = min(tk, 2 * LANES)

    def count_ge(cand):
        c16 = cand.astype(I16)

        def body(j, cnt):
            hit = jnp.where(halves[j] >= c16, jnp.int16(1), jnp.int16(0))
            for c in range(tk // width):
                cnt = cnt + hit[:, c * width:(c + 1) * width]
            return cnt

        cnt = lax.fori_loop(0, nt, body, jnp.zeros((rows, width), I16))
        return jnp.sum(cnt.astype(I32), axis=1, keepdims=True)

    zero = jnp.zeros((rows, 1), I32)
    n_zero = count_ge(zero)
    t = jnp.where(n_zero >= topk, zero, jnp.full((rows, 1), I16_MIN, I32))
    n_t = jnp.where(n_zero >= topk, n_zero, nt * tk)

    def body(b, carry):
        t, n_t = carry
        cand = t | jnp.left_shift(jnp.int32(1), 14 - b)
        n = count_ge(cand)
        return jnp.where(n >= topk, cand, t), jnp.where(n >= topk, n, n_t)

    return lax.fori_loop(0, 15, body, (t, n_t))


def _drop_late_ties(keys, nt, rows, tk, thr, excess):
    col0 = lax.broadcasted_iota(I32, (rows, tk), 1)

    def count(pred):
        def body(j, cnt):
            hit = pred(keys[j], col0 + j * tk).astype(I32)
            for c in range(tk // LANES):
                cnt = cnt + hit[:, c * LANES:(c + 1) * LANES]
            return cnt
        return jnp.sum(lax.fori_loop(0, nt, body, jnp.zeros((rows, LANES), I32)), axis=1, keepdims=True)

    keep = count(lambda k, c: k == thr) - excess
    nbits = max(1, (keys.shape[0] * tk - 1).bit_length())

    def body(b, last):
        cand = last | jnp.left_shift(jnp.int32(1), nbits - 1 - b)
        before = count(lambda k, c: (k == thr) & (c < cand))
        return jnp.where(before <= keep - 1, cand, last)

    last = lax.fori_loop(0, nbits, body, jnp.zeros((rows, 1), I32))

    def drop(j, carry):
        k = keys[j]
        keys[j] = jnp.where((k == thr) & (col0 + j * tk > last), INT_MIN, k)
        return carry

    lax.fori_loop(0, nt, drop, 0)


def _topk_threshold(keys, hi16, lo16, nt, rows, tk, topk):
    top, n_top = _kth_largest_half(hi16, nt, rows, tk, topk)

    def low_tile(j, carry):
        key = keys[j]
        hi = key >> 16
        lo = (key & 0xFFFF) + I16_MIN
        lo16[j] = jnp.where(hi == top, lo, jnp.where(hi > top, I16_MAX, I16_MIN)).astype(I16)
        return carry

    lax.fori_loop(0, nt, low_tile, 0)
    low, n_low = _kth_largest_half(lo16, nt, rows, tk, topk)
    raw = top * 65536 + (low - I16_MIN)
    thr = jnp.maximum(raw, INT_MIN + 1)
    n_ge = jnp.where(low == I16_MIN, n_top, n_low)
    excess = jnp.where(raw == INT_MIN, 0, n_ge - topk)

    @pl.when(jnp.max(excess) > 0)
    def _():
        _drop_late_ties(keys, nt, rows, tk, thr, excess)

    return thr


def _pdsa_kernel(qi_ref, wi_ref, qb_ref, ki_ref, kb_ref, vb_ref, o_ref, keys, hi16, lo16, m_s, l_s, acc,
                 *, tq, tk, topk, online):
    i = pl.program_id(0)
    nt = (i * tq + tq + tk - 1) // tk
    row = i * tq + lax.broadcasted_iota(I32, (tq, tk), 0)
    col0 = lax.broadcasted_iota(I32, (tq, tk), 1)

    qi_st = jnp.concatenate([qi_ref[:, h * D_IDX:(h + 1) * D_IDX] for h in range(H_IDX)], axis=0)
    wv = wi_ref[...]
    w_cols = [wv[:, D_IDX + h:D_IDX + h + 1] for h in range(H_IDX)]

    def score_tile(j):
        start = pl.multiple_of(j * tk, tk)
        lg = _dot_t(qi_st, ki_ref[pl.ds(start, tk), :])
        sc = jnp.zeros((tq, tk), F32)
        for h in range(H_IDX):
            sc = sc + jnp.maximum(lg[h * tq:(h + 1) * tq], 0.0) * w_cols[h]
        key = jnp.where(col0 + j * tk <= row, _sortable_key(sc), INT_MIN)
        keys[j] = key
        hi16[j] = (key >> 16).astype(I16)

    _for_each_tile(nt, score_tile)

    thr = _topk_threshold(keys, hi16, lo16, nt, tq, tk, topk)

    m_s[...] = jnp.full(m_s.shape, NEG_BIG, F32)
    l_s[...] = jnp.zeros(l_s.shape, F32)
    acc[...] = jnp.zeros(acc.shape, F32)
    q_st = [jnp.concatenate([qb_ref[:, (kv * G_B + g) * HD_B:(kv * G_B + g + 1) * HD_B] for g in range(G_B)], axis=0)
            for kv in range(KV_B)]

    def attn_tile(j):
        start = pl.multiple_of(j * tk, tk)
        kt = keys[j]
        sel = kt >= thr
        sel = jnp.concatenate([sel] * G_B, axis=0)
        for kv in range(KV_B):
            s = _dot_t(q_st[kv], kb_ref[pl.ds(start, tk), kv * HD_B:(kv + 1) * HD_B])
            vt = vb_ref[pl.ds(start, tk), kv * HD_B:(kv + 1) * HD_B]
            if online:
                _online_softmax_step(jnp.where(sel, s, -jnp.inf), vt, m_s, l_s, acc, kv)
            else:
                _bounded_softmax_step(s, sel, vt, l_s, acc, kv)

    _for_each_tile(nt, attn_tile)
    for kv in range(KV_B):
        o = acc[kv] / _row_sum(l_s[kv], online)
        for g in range(G_B):
            hg = kv * G_B + g
            o_ref[:, hg * HD_B:(hg + 1) * HD_B] = o[g * tq:(g + 1) * tq]


def _prompt_dsa(qi, wi, qb, ki, kb, vb, tq, tk, online):
    t = qb.shape[0]
    topk = min(TOPK_MAX, t // 4)
    kern = functools.partial(_pdsa_kernel, tq=tq, tk=tk, topk=topk, online=online)
    row = lambda w: pl.BlockSpec((tq, w), lambda i: (i, 0))
    return pl.pallas_call(
        kern,
        out_shape=jax.ShapeDtypeStruct((t, H_B * HD_B), F32),
        grid=(t // tq,),
        in_specs=[row(W_QI), row(LANES), row(W_QB), _const_spec(ki.shape), _const_spec(kb.shape),
                  _const_spec(vb.shape)],
        out_specs=row(H_B * HD_B),
        scratch_shapes=[pltpu.VMEM((t // tk, tq, tk), I32), pltpu.VMEM((t // tk, tq, tk), I16),
                        pltpu.VMEM((t // tk, tq, tk), I16), pltpu.VMEM((KV_B, G_B * tq, LANES), F32),
                        pltpu.VMEM((KV_B, G_B * tq, LANES), F32), pltpu.VMEM((KV_B, G_B * tq, LANES), F32)],
        compiler_params=_cparams(("parallel",)),
        name="prompt_dsa",
    )(qi, wi, qb, ki, kb, vb)


PAGE_SLOTS = 3


def _fetch_pages(pt_ref, pools, bufs, sems, *, pps, nsteps, total):
    t = pl.program_id(0) * nsteps + pl.program_id(1)
    slot = t % PAGE_SLOTS
    ahead = PAGE_SLOTS - 1

    def copies(tt, s, p):
        pg = pt_ref[tt // nsteps, (tt % nsteps) * pps + p]
        return [pltpu.make_async_copy(pool.at[pg], buf.at[s, p], sem.at[s])
                for pool, buf, sem in zip(pools, bufs, sems)]

    def start(tt, s):
        def body(p, carry):
            for cp in copies(tt, s, p):
                cp.start()
            return carry
        lax.fori_loop(0, pps, body, 0)

    @pl.when(t == 0)
    def _():
        for first in range(min(ahead, total)):
            start(first, first)

    @pl.when(t + ahead < total)
    def _():
        start(t + ahead, (t + ahead) % PAGE_SLOTS)

    def wait(p, carry):
        for cp in copies(t, slot, p):
            cp.wait()
        return carry

    lax.fori_loop(0, pps, wait, 0)
    return slot


def _page_scratch(pools, pps):
    bufs = [pltpu.VMEM((PAGE_SLOTS, pps) + pool.shape[1:], pool.dtype) for pool in pools]
    sems = [pltpu.SemaphoreType.DMA((PAGE_SLOTS,)) for _ in pools]
    return bufs + sems


def _sdiff_kernel(pt_ref, lq1, lk1, lq2, lk2, q_ref, kn_ref, vn_ref, kpool, vpool, o_ref,
                  kbuf, vbuf, ksem, vsem, kc, vc, lhs, m_s, l_s, acc, *, pps, page, tn, lam_init, nsteps, total):
    j = pl.program_id(1)

    @pl.when(j == 0)
    def _():
        q = q_ref[0].astype(F32)
        for kv in range(KV_A):
            parts = [_split_components(q[:, (kv * G_A + g) * LANES:(kv * G_A + g + 1) * LANES]) for g in range(G_A)]
            lhs[kv] = jnp.concatenate(parts, axis=0).astype(BF16)
        m_s[...] = jnp.full(m_s.shape, NEG_BIG, F32)
        l_s[...] = jnp.zeros(l_s.shape, F32)
        acc[...] = jnp.zeros(acc.shape, F32)

    slot = _fetch_pages(pt_ref, [kpool, vpool], [kbuf, vbuf], [ksem, vsem], pps=pps, nsteps=nsteps, total=total)
    for p in range(pps):
        for kv in range(KV_A):
            kc[kv, :, p * page:(p + 1) * page] = kbuf[slot, p, kv].reshape(2 * HD_A, page).astype(BF16)
            vc[kv, p * page:(p + 1) * page, :] = vbuf[slot, p, pl.ds(kv, page, stride=KV_A), :].astype(BF16)
    for kv in range(KV_A):
        s = _dot(lhs[kv], kc[kv])
        _online_softmax_step(s, vc[kv], m_s, l_s, acc, kv)

    @pl.when(j == pl.num_programs(1) - 1)
    def _():
        kn = kn_ref[0]
        vn = vn_ref[0]
        lam = _lambda(lq1, lk1, lq2, lk2, lam_init)
        for kv in range(KV_A):
            s = _dot_t(lhs[kv], kn[:, kv * LANES:(kv + 1) * LANES])
            r = lax.broadcasted_iota(I32, s.shape, 0) % tn
            c = lax.broadcasted_iota(I32, s.shape, 1)
            s = jnp.where(c <= r, s, -jnp.inf)
            _online_softmax_step(s, vn[:, kv * LANES:(kv + 1) * LANES], m_s, l_s, acc, kv)
            o = acc[kv] / l_s[kv]
            for g in range(G_A):
                hg = kv * G_A + g
                o_ref[0, :, hg * LANES:(hg + 1) * LANES] = (o[(2 * g) * tn:(2 * g + 1) * tn]
                                                            - lam * o[(2 * g + 1) * tn:(2 * g + 2) * tn])


def _sample_diff_attn(page_table, lams, qa, ka_new, va_new, pool_kt, pool_v, lam_init, pps):
    nb, tn, _ = qa.shape
    n_pages = page_table.shape[1]
    page = pool_kt.shape[-1]
    nsteps = n_pages // pps
    kern = functools.partial(_sdiff_kernel, pps=pps, page=page, tn=tn, lam_init=lam_init, nsteps=nsteps,
                             total=nb * nsteps)
    seq = lambda w: pl.BlockSpec((1, tn, w), lambda b, j, pt: (b, 0, 0))
    cst = lambda shape: pl.BlockSpec(shape, lambda b, j, pt: (0,) * len(shape))
    hbm = pl.BlockSpec(memory_space=pl.ANY)
    rows = 2 * G_A * tn
    return pl.pallas_call(
        kern,
        out_shape=jax.ShapeDtypeStruct((nb, tn, H_A * DV_A), F32),
        grid_spec=pltpu.PrefetchScalarGridSpec(
            num_scalar_prefetch=1,
            grid=(nb, nsteps),
            in_specs=[cst((1, HD_A))] * 4 + [seq(W_QA), seq(W_KA), seq(W_VA), hbm, hbm],
            out_specs=seq(H_A * DV_A),
            scratch_shapes=_page_scratch([pool_kt, pool_v], pps)
                           + [pltpu.VMEM((KV_A, 2 * HD_A, pps * page), BF16),
                              pltpu.VMEM((KV_A, pps * page, DV_A), BF16),
                              pltpu.VMEM((KV_A, rows, LANES), BF16), pltpu.VMEM((KV_A, rows, LANES), F32),
                              pltpu.VMEM((KV_A, rows, LANES), F32), pltpu.VMEM((KV_A, rows, LANES), F32)]),
        compiler_params=_cparams(("arbitrary", "arbitrary")),
        name="sample_diff_attn",
    )(page_table, *lams, qa, ka_new, va_new, pool_kt, pool_v)


def _sscore_kernel(pt_ref, qi_ref, wi_ref, kin_ref, ipool, keys_o, newkeys_o, ibuf, isem,
                   *, pps, page, tn, tile, nsteps, total):
    j = pl.program_id(1)
    q = qi_ref[0].astype(F32)
    qi_st = jnp.concatenate([q[:, h * D_IDX:(h + 1) * D_IDX] for h in range(H_IDX)], axis=0).astype(BF16)
    wv = wi_ref[0]
    w_cols = [wv[:, D_IDX + h:D_IDX + h + 1] for h in range(H_IDX)]

    def scores(lg):
        sc = jnp.zeros((tn, lg.shape[1]), F32)
        for h in range(H_IDX):
            sc = sc + jnp.maximum(lg[h * tn:(h + 1) * tn], 0.0) * w_cols[h]
        return _sortable_key(sc)

    slot = _fetch_pages(pt_ref, [ipool], [ibuf], [isem], pps=pps, nsteps=nsteps, total=total)
    kt = jnp.concatenate([ibuf[slot, p] for p in range(pps)], axis=1).astype(BF16)
    keys = scores(_dot(qi_st, kt))
    for c in range(pps * page // tile):
        keys_o[c] = keys[:, c * tile:(c + 1) * tile]

    @pl.when(j == nsteps - 1)
    def _():
        nk = scores(_dot_t(qi_st, kin_ref[0]))
        r = lax.broadcasted_iota(I32, nk.shape, 0)
        c = lax.broadcasted_iota(I32, nk.shape, 1)
        nk = jnp.where(c <= r, nk, INT_MIN)
        newkeys_o[...] = jnp.concatenate([nk, jnp.full((tn, LANES - tn), INT_MIN, I32)], axis=1)


def _sample_scores(page_table, qi, wi, ki_new, pool_it, pps, tile):
    nb, tn, _ = qi.shape
    n_pages = page_table.shape[1]
    page = pool_it.shape[2]
    nsteps = n_pages // pps
    tiles_per_step = pps * page // tile
    kern = functools.partial(_sscore_kernel, pps=pps, page=page, tn=tn, tile=tile, nsteps=nsteps,
                             total=nb * nsteps)
    seq = lambda w: pl.BlockSpec((1, tn, w), lambda b, j, pt: (b, 0, 0))
    return pl.pallas_call(
        kern,
        out_shape=[jax.ShapeDtypeStruct((n_pages * page // tile, nb * tn, tile), I32),
                   jax.ShapeDtypeStruct((nb * tn, LANES), I32)],
        grid_spec=pltpu.PrefetchScalarGridSpec(
            num_scalar_prefetch=1,
            grid=(nb, nsteps),
            in_specs=[seq(W_QI), seq(LANES), seq(D_IDX), pl.BlockSpec(memory_space=pl.ANY)],
            out_specs=[pl.BlockSpec((tiles_per_step, tn, tile), lambda b, j, pt: (j, b, 0)),
                       pl.BlockSpec((tn, LANES), lambda b, j, pt: (b, 0))],
            scratch_shapes=_page_scratch([pool_it], pps)),
        compiler_params=_cparams(("arbitrary", "arbitrary")),
        name="sample_scores",
    )(page_table, qi, wi, ki_new, pool_it)


def _sthr_kernel(keys_ref, newkeys_ref, sel_ref, keys, hi16, lo16, *, rows, tile, nt, topk):
    for j in range(nt - 1):
        keys[j] = keys_ref[j]
    pad = [jnp.full((rows, tile - LANES), INT_MIN, I32)] if tile > LANES else []
    keys[nt - 1] = jnp.concatenate([newkeys_ref[...]] + pad, axis=1)

    def split(j, carry):
        hi16[j] = (keys[j] >> 16).astype(I16)
        return carry

    lax.fori_loop(0, nt, split, 0)
    thr = _topk_threshold(keys, hi16, lo16, nt, rows, tile, topk)
    for j in range(nt - 1):
        sel_ref[:, j * tile:(j + 1) * tile] = (keys[j] >= thr).astype(F32)
    sel_ref[:, (nt - 1) * tile:] = (keys[nt - 1][:, :LANES] >= thr).astype(F32)


def _sample_select(keys, newkeys, topk):
    n_tiles, rows_all, tile = keys.shape
    rows = _tile(rows_all, 128)
    nt = n_tiles + 1
    kern = functools.partial(_sthr_kernel, rows=rows, tile=tile, nt=nt, topk=topk)
    width = n_tiles * tile + LANES
    return pl.pallas_call(
        kern,
        out_shape=jax.ShapeDtypeStruct((rows_all, width), F32),
        grid=(rows_all // rows,),
        in_specs=[pl.BlockSpec((n_tiles, rows, tile), lambda i: (0, i, 0)),
                  pl.BlockSpec((rows, LANES), lambda i: (i, 0))],
        out_specs=pl.BlockSpec((rows, width), lambda i: (i, 0)),
        scratch_shapes=[pltpu.VMEM((nt, rows, tile), I32), pltpu.VMEM((nt, rows, tile), I16),
                        pltpu.VMEM((nt, rows, tile), I16)],
        compiler_params=_cparams(("parallel",)),
        name="sample_select",
    )(keys, newkeys)


def _sdsa_kernel(pt_ref, q_ref, sel_ref, seln_ref, kn_ref, vn_ref, kpool, vpool, o_ref,
                 kbuf, vbuf, ksem, vsem, kc, vc, lhs, m_s, l_s, acc, *, pps, page, tn, nsteps, total):
    j = pl.program_id(1)

    @pl.when(j == 0)
    def _():
        q = q_ref[0].astype(F32)
        for kv in range(KV_B):
            parts = [q[:, (kv * G_B + g) * HD_B:(kv * G_B + g + 1) * HD_B] for g in range(G_B)]
            lhs[kv] = jnp.concatenate(parts, axis=0).astype(BF16)
        m_s[...] = jnp.full(m_s.shape, NEG_BIG, F32)
        l_s[...] = jnp.zeros(l_s.shape, F32)
        acc[...] = jnp.zeros(acc.shape, F32)

    def step(sel, kmats, vmats):
        sel = jnp.concatenate([sel > 0.5] * G_B, axis=0)
        for kv in range(KV_B):
            s = jnp.where(sel, _dot_t(lhs[kv], kmats[kv]), -jnp.inf)
            _online_softmax_step(s, vmats[kv], m_s, l_s, acc, kv)

    slot = _fetch_pages(pt_ref, [kpool, vpool], [kbuf, vbuf], [ksem, vsem], pps=pps, nsteps=nsteps, total=total)
    for p in range(pps):
        for kv in range(KV_B):
            kc[kv, p * page:(p + 1) * page, :] = kbuf[slot, p, pl.ds(kv, page, stride=KV_B), :].astype(BF16)
            vc[kv, p * page:(p + 1) * page, :] = vbuf[slot, p, pl.ds(kv, page, stride=KV_B), :].astype(BF16)
    step(sel_ref[0], [kc[kv] for kv in range(KV_B)], [vc[kv] for kv in range(KV_B)])

    @pl.when(j == nsteps - 1)
    def _():
        kn, vn = kn_ref[0], vn_ref[0]
        step(seln_ref[0][:, :tn], [kn[:, kv * HD_B:(kv + 1) * HD_B] for kv in range(KV_B)],
             [vn[:, kv * HD_B:(kv + 1) * HD_B] for kv in range(KV_B)])
        for kv in range(KV_B):
            o = acc[kv] / l_s[kv]
            for g in range(G_B):
                hg = kv * G_B + g
                o_ref[0, :, hg * HD_B:(hg + 1) * HD_B] = o[g * tn:(g + 1) * tn]


def _sample_dsa(page_table, qb, sel, kb_new, vb_new, pool_k, pool_v, pps):
    nb, tn, _ = qb.shape
    n_pages = page_table.shape[1]
    page = pool_k.shape[1] // KV_B
    tk = pps * page
    nsteps = n_pages // pps
    kern = functools.partial(_sdsa_kernel, pps=pps, page=page, tn=tn, nsteps=nsteps, total=nb * nsteps)
    seq = lambda w: pl.BlockSpec((1, tn, w), lambda b, j, pt: (b, 0, 0))
    hbm = pl.BlockSpec(memory_space=pl.ANY)
    rows = G_B * tn
    return pl.pallas_call(
        kern,
        out_shape=jax.ShapeDtypeStruct((nb, tn, H_B * HD_B), F32),
        grid_spec=pltpu.PrefetchScalarGridSpec(
            num_scalar_prefetch=1,
            grid=(nb, nsteps),
            in_specs=[seq(W_QB), pl.BlockSpec((1, tn, tk), lambda b, j, pt: (b, 0, j)),
                      pl.BlockSpec((1, tn, LANES), lambda b, j, pt: (b, 0, (n_pages * page) // LANES)),
                      seq(W_KB), seq(W_KB), hbm, hbm],
            out_specs=seq(H_B * HD_B),
            scratch_shapes=_page_scratch([pool_k, pool_v], pps)
                           + [pltpu.VMEM((KV_B, tk, HD_B), BF16), pltpu.VMEM((KV_B, tk, HD_B), BF16),
                              pltpu.VMEM((KV_B, rows, LANES), BF16), pltpu.VMEM((KV_B, rows, LANES), F32),
                              pltpu.VMEM((KV_B, rows, LANES), F32), pltpu.VMEM((KV_B, rows, LANES), F32)]),
        compiler_params=_cparams(("arbitrary", "arbitrary")),
        name="sample_dsa",
    )(page_table, qb, sel, sel, kb_new, vb_new, pool_k, pool_v)


def _finish_kernel(x_ref, oa_ref, ob_ref, ga_ref, gb_ref, sub_ref, gm_ref, wa, wb, wo, wu, wd, y_ref, *, lam_init):
    oa = oa_ref[...]
    sub = sub_ref[...]
    chunks = []
    for h in range(H_A):
        c = oa[:, h * DV_A:(h + 1) * DV_A]
        ms = jnp.mean(c * c, axis=-1, keepdims=True)
        chunks.append(((c * lax.rsqrt(ms + EPS)) * sub) * (1.0 - lam_init))
    oan = jnp.concatenate(chunks, axis=1).astype(BF16)
    pa = _dot(oan, wa[...])
    pb = _dot(ob_ref[...].astype(BF16), wb[...])
    mix = (ga_ref[...] * pa + gb_ref[...] * pb).astype(BF16)
    x1 = x_ref[...] + _dot(mix, wo[...])
    ms = jnp.mean(x1 * x1, axis=-1, keepdims=True)
    xn = ((x1 * lax.rsqrt(ms + EPS)) * gm_ref[...]).astype(BF16)
    hid = jnp.maximum(_dot(xn, wu[...]), 0.0)
    y_ref[...] = x1 + _dot((hid * hid).astype(BF16), wd[...])


def _finish(x, oa, ob, ga, gb, pw, lam_init, tm):
    t, d = x.shape
    row = pl.BlockSpec((tm, d), lambda i: (i, 0))
    weights = [pw[k] for k in ("w_a_proj", "w_b_proj", "w_out", "w_up", "w_down")]
    kern = functools.partial(_finish_kernel, lam_init=lam_init)
    return pl.pallas_call(
        kern,
        out_shape=jax.ShapeDtypeStruct((t, d), F32),
        grid=(t // tm,),
        in_specs=[row] * 5 + [_const_spec((1, DV_A)), _const_spec((1, d))] + [_const_spec(w.shape) for w in weights],
        out_specs=row,
        compiler_params=_cparams(("parallel",)),
        name="finish",
    )(x, oa, ob, ga, gb, pw["sub_norm_a"], pw["g_mlp"], *weights)


def _prep_weights(l, g_attn_norm, w_in, q_norm_a, k_norm_a, sub_norm_a, q_norm_b, k_norm_b, idx_k_norm,
                  w_a_proj, w_b_proj, w_out, g_mlp_norm, w_up, w_down):
    w = w_in[l].astype(BF16)
    widths = (W_QA, W_KA, W_VA, W_QB, W_KB, W_KB, W_QI, D_IDX, H_IDX, w.shape[0], w.shape[0])
    offs = np.concatenate([[0], np.cumsum(widths)])
    seg = [w[:, int(offs[i]):int(offs[i + 1])] for i in range(len(widths))]
    wkw = jnp.concatenate([seg[7], seg[8], jnp.zeros((w.shape[0], LANES - D_IDX - H_IDX), BF16)], axis=1)
    lane = np.arange(LANES)
    row2 = lambda v: v.astype(F32)[None, :]
    return {
        "g_attn": row2(g_attn_norm[l]),
        "wqa": seg[0], "wka": seg[1], "wva": seg[2], "wqb": seg[3], "wkb": seg[4], "wvb": seg[5],
        "wqi": seg[6], "wkw": wkw, "wga": seg[9], "wgb": seg[10],
        "gqa": row2(jnp.tile(q_norm_a[l], W_QA // HD_A) * (HD_A ** -0.5 * LOG2E)),
        "gka": row2(jnp.tile(k_norm_a[l], W_KA // HD_A)),
        "gqb": row2(jnp.tile(q_norm_b[l], W_QB // HD_B) * (HD_B ** -0.5 * LOG2E)),
        "gkb": row2(jnp.tile(k_norm_b[l], W_KB // HD_B)),
        "gki": row2(jnp.concatenate([idx_k_norm[l], jnp.zeros((LANES - D_IDX,), F32)])),
        "ones64": jnp.asarray((lane[:, None] // HD_A) == (lane[None, :] // HD_A), BF16),
        "ones128": jnp.ones((LANES, LANES), BF16),
        "sub_norm_a": row2(sub_norm_a[l]), "g_mlp": row2(g_mlp_norm[l]),
        "w_a_proj": w_a_proj[l].astype(BF16), "w_b_proj": w_b_proj[l].astype(BF16),
        "w_out": w_out[l].astype(BF16), "w_up": w_up[l].astype(BF16), "w_down": w_down[l].astype(BF16),
    }


def _tile(n, pref):
    return pref if n % pref == 0 else n


def _tiles(seq, n_sample_rows, n_pages):
    pps = next(p for p in (32, 16, 8, 4, 2, 1) if n_pages % p == 0)
    return {"proj": _tile(seq, 256), "diff": _tile(seq, 256), "diff_k": _tile(seq, 512),
            "dsa_q": _tile(seq, 128), "dsa_k": _tile(seq, 512),
            "finish": _tile(seq, 256), "sample_rows": _tile(n_sample_rows, 256), "pps": pps,
            "sel_tile": _tile(pps * 128, 512)}


def _score_bound(q_gain, k_gain, head_dim):
    return (head_dim ** 0.5) * jnp.max(jnp.abs(q_gain)) * jnp.max(jnp.abs(k_gain))


def kernel(x_prompt, x_sample, cache_a_k, cache_a_v, cache_b_k, cache_b_v, cache_b_idx_k, page_table, g_attn_norm, w_in, q_norm_a, k_norm_a, lam_q1, lam_k1, lam_q2, lam_k2, sub_norm_a, q_norm_b, k_norm_b, idx_k_norm, w_a_proj, w_b_proj, w_out, g_mlp_norm, w_up, w_down):
    depth = w_in.shape[0]
    nb_p, seq, d = x_prompt.shape
    nb_s, tn, _ = x_sample.shape
    n_pages = page_table.shape[1]
    n_phys, page = cache_a_k.shape[1], cache_a_k.shape[2]
    past = n_pages * page
    assert nb_p == 1
    pos_p = jnp.arange(seq)
    pos_s = past + (jnp.arange(nb_s * tn) % tn)
    tl = _tiles(seq, nb_s * tn, n_pages)
    pps = tl["pps"]

    xp = x_prompt.reshape(seq, d)
    xs = x_sample.reshape(nb_s * tn, d)
    outs_p = [[] for _ in range(5)]
    outs_s = [[] for _ in range(5)]
    for l in range(depth):
        lam_init = 0.8 - 0.6 * math.exp(-0.3 * l)
        pw = _prep_weights(l, g_attn_norm, w_in, q_norm_a, k_norm_a, sub_norm_a, q_norm_b, k_norm_b, idx_k_norm,
                           w_a_proj, w_b_proj, w_out, g_mlp_norm, w_up, w_down)
        lams = [v[l].astype(F32)[None, :] for v in (lam_q1, lam_k1, lam_q2, lam_k2)]

        (qa, kaf, kab, vaf, vab, qb, kbf, kbb, vbf, vbb, qi, kif, kib, wi, ga, gb) = _project(
            xs, pos_s, pw, tl["sample_rows"])
        seq3 = lambda a: a.reshape(nb_s, tn, a.shape[-1])
        pool_akt = jnp.transpose(cache_a_k[l], (0, 2, 3, 4, 1))
        pool_it = jnp.transpose(cache_b_idx_k[l], (0, 2, 1))
        rows2 = lambda c: c[l].reshape(n_phys, page * c.shape[3], c.shape[4])
        oa = _sample_diff_attn(page_table, lams, seq3(qa), seq3(kab), seq3(vab), pool_akt, rows2(cache_a_v),
                               lam_init, pps)
        keys, newkeys = _sample_scores(page_table, seq3(qi), seq3(wi), seq3(kib), pool_it, pps, tl["sel_tile"])
        sel = seq3(_sample_select(keys, newkeys, min(TOPK_MAX, (past + tn) // 4)))
        ob = _sample_dsa(page_table, seq3(qb), sel, seq3(kbb), seq3(vbb), rows2(cache_b_k), rows2(cache_b_v), pps)
        xs = _finish(xs, oa.reshape(nb_s * tn, -1), ob.reshape(nb_s * tn, -1), ga, gb, pw, lam_init,
                     tl["sample_rows"])
        for lst, v in zip(outs_s, (kaf.reshape(nb_s, tn, KV_A, 2, HD_A), vaf.reshape(nb_s, tn, KV_A, DV_A),
                                   kbf.reshape(nb_s, tn, KV_B, HD_B), vbf.reshape(nb_s, tn, KV_B, HD_B),
                                   kif.reshape(nb_s, tn, D_IDX))):
            lst.append(v)

        (qa, kaf, kab, vaf, vab, qb, kbf, kbb, vbf, vbb, qi, kif, kib, wi, ga, gb) = _project(
            xp, pos_p, pw, tl["proj"])
        diff = lambda online: functools.partial(_prompt_diff_attn, lams, qa, kab, vab, lam_init, tl["diff"], tl["diff_k"],
                                                online)
        oa = lax.cond(_score_bound(q_norm_a[l], k_norm_a[l], HD_A) <= MAX_BOUNDED_SCORE, diff(False), diff(True))
        dsa = lambda online: functools.partial(_prompt_dsa, qi, wi, qb, kib, kbb, vbb, tl["dsa_q"], tl["dsa_k"],
                                               online)
        ob = lax.cond(_score_bound(q_norm_b[l], k_norm_b[l], HD_B) <= MAX_BOUNDED_SCORE, dsa(False), dsa(True))
        xp = _finish(xp, oa, ob, ga, gb, pw, lam_init, tl["finish"])
        for lst, v in zip(outs_p, (kaf.reshape(1, seq, KV_A, 2, HD_A), vaf.reshape(1, seq, KV_A, DV_A),
                                   kbf.reshape(1, seq, KV_B, HD_B), vbf.reshape(1, seq, KV_B, HD_B),
                                   kif.reshape(1, seq, D_IDX))):
            lst.append(v)

    return (xp.reshape(nb_p, seq, d), xs.reshape(nb_s, tn, d),
            *[jnp.stack(v) for v in outs_p], *[jnp.stack(v) for v in outs_s])
```
